```python
import functools
import numpy as np
import jax
import jax.numpy as jnp
from jax import lax

D_MODEL = 1024
BATCH = 4
SEQ = 4096
DEPTH = 1
DEC_BATCH = 128
DEC_SEQ = 1
PAST_LEN = 2048
PAGE_SIZE = 128

NSA_HEADS = 16
NSA_HEAD_DIM = 64
NSA_KV_HEADS = 4
NSA_GROUP = NSA_HEADS // NSA_KV_HEADS
NSA_WIDTH = NSA_HEADS * NSA_HEAD_DIM
NSA_KV_WIDTH = NSA_KV_HEADS * NSA_HEAD_DIM
CMP_STRIDE = 16
CMP_LEN = 2 * CMP_STRIDE
CMP_HIDDEN = 2 * NSA_HEAD_DIM
SLC_BLOCK = 64
SLC_TOPN = 16
WINDOW = 512
Q_BLOCK = 64

DN_HEADS = 8
DN_DK = 128
DN_DV = 128
DN_WIDTH = DN_HEADS * DN_DV
DN_QKV = 2 * DN_HEADS * DN_DK + DN_WIDTH
CONV_W = 4
DN_CHUNK = 64

D_FF = 4 * D_MODEL
N_MOD = 6
EPS = 1e-6
IN_SPLITS = (NSA_WIDTH, 6 * NSA_KV_WIDTH, 3 * NSA_HEADS, DN_QKV, DN_WIDTH, DN_HEADS, DN_HEADS, 2 * D_MODEL)
IN_WIDTH = sum(IN_SPLITS)

kernel_name = 'nsa_gated_deltanet_hybrid_step'


def _rmsnorm(x, g):
    xf = x.astype(jnp.float32)
    y = xf * lax.rsqrt(jnp.mean(xf * xf, axis=-1, keepdims=True) + EPS)
    return (y * g.astype(jnp.float32)).astype(x.dtype)


def _l2norm(x):
    xf = x.astype(jnp.float32)
    return xf * lax.rsqrt(jnp.sum(xf * xf, axis=-1, keepdims=True) + EPS)


def _masked_softmax(s, mask):
    s = jnp.where(mask, s.astype(jnp.float32), -jnp.inf)
    m = jnp.max(s, axis=-1, keepdims=True)
    m = jnp.where(jnp.isfinite(m), m, 0.0)
    p = jnp.exp(s - m)
    return p / jnp.maximum(jnp.sum(p, axis=-1, keepdims=True), 1e-30)


def _compress(k, pe, w1, w2):
    n, lp, g, d = k.shape
    kc = k.reshape(n, lp // CMP_STRIDE, CMP_STRIDE, g, d)
    first = jnp.einsum('ncpgd,pde->ncge', kc, w1[:CMP_STRIDE])
    second = jnp.einsum('ncpgd,pde->ncge', kc, w1[CMP_STRIDE:])
    pe_term = jnp.einsum('pd,pde->e', pe, w1)
    hid = jax.nn.gelu(first[:, :-1] + second[:, 1:] + pe_term)
    return jnp.einsum('ncge,ed->ncgd', hid, w2)


def _block_overlap(n_cmp, n_blk):
    c0 = np.arange(n_cmp)[:, None] * CMP_STRIDE
    b0 = np.arange(n_blk)[None, :] * SLC_BLOCK
    inter = np.minimum(c0 + CMP_LEN, b0 + SLC_BLOCK) - np.maximum(c0, b0)
    return jnp.asarray(np.clip(inter, 0, None) / CMP_LEN, dtype=jnp.float32)


def _to_blocks(k):
    n, lp, g, d = k.shape
    return k.reshape(n, lp // SLC_BLOCK, SLC_BLOCK, g, d).transpose(0, 3, 1, 2, 4)


def _nsa_core(q, q_pos, kc, vc, kc_end, ks_blk, vs_blk, kw, vw, kw_pos, gates, overlap):
    scale = NSA_HEAD_DIM ** -0.5
    n, nq, g, hg, _ = q.shape
    n_blk = ks_blk.shape[2]
    top_n = min(SLC_TOPN, n_blk)
    qp = q_pos[:, None]
    s = jnp.einsum('nqghd,ncgd->nqghc', q, kc) * scale
    p_cmp = _masked_softmax(s, (kc_end[None, :] <= qp)[None, :, None, None, :])
    o_cmp = jnp.einsum('nqghc,ncgd->nqghd', p_cmp, vc)
    imp = jnp.einsum('nqghc,cj->nqgj', p_cmp, overlap)
    blk = jnp.arange(n_blk)[None, :]
    forced = (blk == (q_pos // SLC_BLOCK)[:, None]) | (blk == 0)
    causal = blk * SLC_BLOCK <= qp
    imp = jnp.where(forced[None, :, None, :], jnp.inf,
                    jnp.where(causal[None, :, None, :], imp, -jnp.inf))
    _, idx = lax.top_k(imp, top_n)
    n_i = jnp.arange(n)[:, None, None, None]
    g_i = jnp.arange(g)[None, None, :, None]
    ksel = ks_blk[n_i, g_i, idx]
    vsel = vs_blk[n_i, g_i, idx]
    kpos = idx[..., None] * SLC_BLOCK + jnp.arange(SLC_BLOCK)
    s = jnp.einsum('nqghd,nqgksd->nqghks', q, ksel) * scale
    s = s.reshape(n, nq, g, hg, top_n * SLC_BLOCK)
    smask = (kpos <= q_pos[None, :, None, None, None]).reshape(n, nq, g, 1, top_n * SLC_BLOCK)
    p = _masked_softmax(s, smask).reshape(n, nq, g, hg, top_n, SLC_BLOCK)
    o_slc = jnp.einsum('nqghks,nqgksd->nqghd', p, vsel)
    s = jnp.einsum('nqghd,nlgd->nqghl', q, kw) * scale
    kp = kw_pos[None, :]
    wmask = (kp <= qp) & (kp > qp - WINDOW) & (kp >= 0)
    p = _masked_softmax(s, wmask[None, :, None, None, :])
    o_win = jnp.einsum('nqghl,nlgd->nqghd', p, vw)
    out = gates[..., 0:1] * o_cmp + gates[..., 1:2] * o_slc + gates[..., 2:3] * o_win
    return out.astype(q.dtype)


def _nsa_prompt(q, kv, gates, cmp_params):
    k_cmp, v_cmp, k_slc, v_slc, k_win, v_win = kv
    pe_k, w1_k, w2_k, pe_v, w1_v, w2_v = cmp_params
    b, s = q.shape[:2]
    kc = _compress(k_cmp, pe_k, w1_k, w2_k)
    vc = _compress(v_cmp, pe_v, w1_v, w2_v)
    n_cmp = kc.shape[1]
    kc_end = jnp.arange(n_cmp) * CMP_STRIDE + (CMP_LEN - 1)
    overlap = _block_overlap(n_cmp, s // SLC_BLOCK)
    ks_blk = _to_blocks(k_slc)
    vs_blk = _to_blocks(v_slc)
    pad = ((0, 0), (WINDOW, 0), (0, 0), (0, 0))
    kw_pad = jnp.pad(k_win, pad)
    vw_pad = jnp.pad(v_win, pad)

    def body(i):
        s0 = i * Q_BLOCK
        qb = lax.dynamic_slice_in_dim(q, s0, Q_BLOCK, axis=1)
        gb = lax.dynamic_slice_in_dim(gates, s0, Q_BLOCK, axis=1)
        kw = lax.dynamic_slice_in_dim(kw_pad, s0, WINDOW + Q_BLOCK, axis=1)
        vw = lax.dynamic_slice_in_dim(vw_pad, s0, WINDOW + Q_BLOCK, axis=1)
        q_pos = s0 + jnp.arange(Q_BLOCK)
        kw_pos = s0 - WINDOW + jnp.arange(WINDOW + Q_BLOCK)
        return _nsa_core(qb, q_pos, kc, vc, kc_end, ks_blk, vs_blk, kw, vw, kw_pos, gb, overlap)

    o = lax.map(body, jnp.arange(s // Q_BLOCK))
    o = jnp.moveaxis(o, 0, 1).reshape(b, s, NSA_WIDTH)
    wlen = min(WINDOW, s)
    return o, (k_cmp, v_cmp, k_slc, v_slc, k_win[:, s - wlen:], v_win[:, s - wlen:])


def _nsa_sample(q, kv, gates, cmp_params, caches, page_table):
    k_cmp, v_cmp, k_slc, v_slc, k_win, v_win = kv
    pe_k, w1_k, w2_k, pe_v, w1_v, w2_v = cmp_params
    pool_k_cmp, pool_v_cmp, pool_k_slc, pool_v_slc, buf_k_win, buf_v_win = caches
    n, t = q.shape[:2]
    past = page_table.shape[1] * pool_k_cmp.shape[1]
    total = past + t
    lp = -(-total // SLC_BLOCK) * SLC_BLOCK

    def full(pool, new):
        rows = pool[page_table].reshape(n, past, NSA_KV_HEADS, NSA_HEAD_DIM)
        seq = jnp.concatenate([rows.astype(new.dtype), new], axis=1)
        return jnp.pad(seq, ((0, 0), (0, lp - total), (0, 0), (0, 0)))

    kc = _compress(full(pool_k_cmp, k_cmp), pe_k, w1_k, w2_k)
    vc = _compress(full(pool_v_cmp, v_cmp), pe_v, w1_v, w2_v)
    n_cmp = kc.shape[1]
    kc_end = jnp.arange(n_cmp) * CMP_STRIDE + (CMP_LEN - 1)
    overlap = _block_overlap(n_cmp, lp // SLC_BLOCK)
    ks_blk = _to_blocks(full(pool_k_slc, k_slc))
    vs_blk = _to_blocks(full(pool_v_slc, v_slc))
    wbuf = buf_k_win.shape[1]
    kw = jnp.concatenate([buf_k_win.astype(k_win.dtype), k_win], axis=1)
    vw = jnp.concatenate([buf_v_win.astype(v_win.dtype), v_win], axis=1)
    kw_pos = past - wbuf + jnp.arange(wbuf + t)
    q_pos = past + jnp.arange(t)
    o = _nsa_core(q, q_pos, kc, vc, kc_end, ks_blk, vs_blk, kw, vw, kw_pos, gates, overlap)
    o = o.reshape(n, t, NSA_WIDTH)
    return o, (k_cmp, v_cmp, k_slc, v_slc, kw[:, t:], vw[:, t:])


def _dn_features(qkv, conv_buf, conv_w, a, b, a_log, dt_bias):
    n, l = qkv.shape[:2]
    xc = jnp.concatenate([conv_buf.astype(qkv.dtype), qkv], axis=1)
    y = lax.conv_general_dilated(xc, conv_w[:, None, :].astype(xc.dtype), window_strides=(1,),
                                 padding='VALID', dimension_numbers=('NWC', 'WIO', 'NWC'),
                                 feature_group_count=DN_QKV)
    y = jax.nn.silu(y)
    q, k, v = jnp.split(y, [DN_HEADS * DN_DK, 2 * DN_HEADS * DN_DK], axis=-1)
    q = _l2norm(q.reshape(n, l, DN_HEADS, DN_DK))
    k = _l2norm(k.reshape(n, l, DN_HEADS, DN_DK))
    v = v.reshape(n, l, DN_HEADS, DN_DV).astype(jnp.float32)
    g = -jnp.exp(a_log.astype(jnp.float32)) * jax.nn.softplus(a.astype(jnp.float32) + dt_bias.astype(jnp.float32))
    beta = jax.nn.sigmoid(b.astype(jnp.float32))
    return q, k, v, g, beta, xc[:, xc.shape[1] - (CONV_W - 1):]


def _gated_delta(q, k, v, g, beta, s0):
    n, l = q.shape[:2]
    c = min(DN_CHUNK, l)
    lp = -(-l // c) * c

    def chunks(x):
        x = jnp.pad(x.astype(jnp.float32), [(0, 0), (0, lp - l)] + [(0, 0)] * (x.ndim - 2))
        x = x.reshape((n, lp // c, c) + x.shape[2:])
        return jnp.swapaxes(jnp.swapaxes(x, 2, 3), 0, 1)

    q = chunks(q) * DN_DK ** -0.5
    k, v, g, beta = chunks(k), chunks(v), chunks(g), chunks(beta)
    gc = jnp.cumsum(g, axis=-1)
    idx = jnp.arange(c)
    incl = idx[:, None] >= idx[None, :]
    strict = idx[:, None] > idx[None, :]
    decay = jnp.exp(jnp.where(incl, gc[..., :, None] - gc[..., None, :], -jnp.inf))
    kb = k * beta[..., None]
    lmat = jnp.where(strict, jnp.einsum('...id,...jd->...ij', kb, k) * decay, 0.0)
    eye = jnp.eye(c, dtype=jnp.float32)
    tmat = lax.linalg.triangular_solve(eye + lmat, jnp.broadcast_to(eye, lmat.shape),
                                       left_side=True, lower=True, unit_diagonal=True)
    u = tmat @ (v * beta[..., None])
    w = tmat @ (kb * jnp.exp(gc)[..., None])
    a_intra = jnp.einsum('...id,...jd->...ij', q, k) * decay
    q_dec = q * jnp.exp(gc)[..., None]
    g_last = gc[..., -1]
    k_dec = k * jnp.exp(g_last[..., None] - gc)[..., None]

    def step(state, xs):
        q_c, k_c, u_c, w_c, a_c, gl_c = xs
        v_new = u_c - w_c @ state
        o = q_c @ state + a_c @ v_new
        state = state * jnp.exp(gl_c)[..., None, None] + jnp.swapaxes(k_c, -1, -2) @ v_new
        return state, o

    s_fin, o = lax.scan(step, s0.astype(jnp.float32), (q_dec, k_dec, u, w, a_intra, g_last))
    o = jnp.swapaxes(jnp.swapaxes(o, 0, 1), 2, 3).reshape(n, lp, DN_HEADS, DN_DV)[:, :l]
    return o, s_fin


def _dn_out(o, z, norm_g):
    n, l = o.shape[:2]
    on = o * lax.rsqrt(jnp.mean(o * o, axis=-1, keepdims=True) + EPS) * norm_g.astype(jnp.float32)
    zf = z.reshape(n, l, DN_HEADS, DN_DV).astype(jnp.float32)
    return (on * jax.nn.silu(zf)).reshape(n, l, DN_WIDTH).astype(z.dtype)


def _layer(x, c, nsa_fn, conv_buf, dn_state, w_ada, b_ada, norm1_g, norm2_g, w_in, dn_conv_w,
           dn_a_log, dn_dt_bias, dn_norm_g, w_out, w_up, w_down):
    n, l, _ = x.shape
    mod = (c @ w_ada + b_ada)[:, None, :]
    sh1, sc1, gt1, sh2, sc2, gt2 = jnp.split(mod, N_MOD, axis=-1)
    h = _rmsnorm(x, norm1_g) * (1.0 + sc1) + sh1
    proj = h @ w_in
    cuts = [int(v) for v in np.cumsum(IN_SPLITS)[:-1]]
    q, kv, nsa_g, qkv, z, a, b, merge = jnp.split(proj, cuts, axis=-1)
    q = q.reshape(n, l, NSA_KV_HEADS, NSA_GROUP, NSA_HEAD_DIM)
    kv = [t.reshape(n, l, NSA_KV_HEADS, NSA_HEAD_DIM) for t in jnp.split(kv, 6, axis=-1)]
    nsa_g = jax.nn.sigmoid(nsa_g.reshape(n, l, NSA_KV_HEADS, NSA_GROUP, 3))
    o_nsa, nsa_state = nsa_fn(q, kv, nsa_g)
    dq, dk, dv, g, beta, conv_new = _dn_features(qkv, conv_buf, dn_conv_w, a, b, dn_a_log, dn_dt_bias)
    o_dn, dn_new = _gated_delta(dq, dk, dv, g, beta, dn_state)
    o_dn = _dn_out(o_dn, z, dn_norm_g)
    gate_a, gate_b = jnp.split(jax.nn.sigmoid(merge), 2, axis=-1)
    mixed = (gate_a * o_nsa.astype(x.dtype) + gate_b * o_dn.astype(x.dtype)) @ w_out
    x = x + gt1 * mixed
    h = _rmsnorm(x, norm2_g) * (1.0 + sc2) + sh2
    x = x + gt2 * (jnp.square(jax.nn.relu(h @ w_up)) @ w_down)
    return x, nsa_state, conv_new, dn_new


def setup_inputs(seed: int = 0) -> dict:
    key = jax.random.key(seed)
    ks = iter(jax.random.split(key, 48))

    def nrm(shape, scale):
        return jax.random.normal(next(ks), shape, jnp.float32) * scale

    n_pages = PAST_LEN // PAGE_SIZE
    n_pool = (DEC_BATCH * n_pages * 5) // 4
    wbuf = min(WINDOW, PAST_LEN)
    pool_shape = (DEPTH, n_pool, PAGE_SIZE, NSA_KV_HEADS, NSA_HEAD_DIM)
    win_shape = (DEPTH, DEC_BATCH, wbuf, NSA_KV_HEADS, NSA_HEAD_DIM)
    page_table = jax.random.permutation(next(ks), n_pool)[:DEC_BATCH * n_pages]
    page_table = page_table.reshape(DEC_BATCH, n_pages).astype(jnp.int32)
    a_log = jnp.log(jax.random.uniform(next(ks), (DEPTH, DN_HEADS), jnp.float32, 1.0, 16.0))
    dt = jnp.exp(jax.random.uniform(next(ks), (DEPTH, DN_HEADS), jnp.float32, np.log(1e-3), np.log(1e-1)))
    dt_bias = jnp.log(jnp.expm1(dt))
    return {
        'x_prompt': nrm((BATCH, SEQ, D_MODEL), 1.0),
        'x_sample': nrm((DEC_BATCH, DEC_SEQ, D_MODEL), 1.0),
        'c_prompt': nrm((BATCH, D_MODEL), 1.0),
        'c_sample': nrm((DEC_BATCH, D_MODEL), 1.0),
        'cache_k_cmp': nrm(pool_shape, 1.0),
        'cache_v_cmp': nrm(pool_shape, 1.0),
        'cache_k_slc': nrm(pool_shape, 1.0),
        'cache_v_slc': nrm(pool_shape, 1.0),
        'cache_k_win': nrm(win_shape, 1.0),
        'cache_v_win': nrm(win_shape, 1.0),
        'state_conv': nrm((DEPTH, DEC_BATCH, CONV_W - 1, DN_QKV), 1.0),
        'state_dn': nrm((DEPTH, DEC_BATCH, DN_HEADS, DN_DK, DN_DV), 0.3),
        'page_table': page_table,
        'w_ada': nrm((DEPTH, D_MODEL, N_MOD * D_MODEL), 0.5 * D_MODEL ** -0.5),
        'b_ada': nrm((DEPTH, N_MOD * D_MODEL), 0.01),
        'norm1_g': 1.0 + nrm((DEPTH, D_MODEL), 0.02),
        'norm2_g': 1.0 + nrm((DEPTH, D_MODEL), 0.02),
        'w_in': nrm((DEPTH, D_MODEL, IN_WIDTH), D_MODEL ** -0.5),
        'cmp_pe_k': nrm((DEPTH, CMP_LEN, NSA_HEAD_DIM), 0.1),
        'cmp_w1_k': nrm((DEPTH, CMP_LEN, NSA_HEAD_DIM, CMP_HIDDEN), (CMP_LEN * NSA_HEAD_DIM) ** -0.5),
        'cmp_w2_k': nrm((DEPTH, CMP_HIDDEN, NSA_HEAD_DIM), CMP_HIDDEN ** -0.5),
        'cmp_pe_v': nrm((DEPTH, CMP_LEN, NSA_HEAD_DIM), 0.1),
        'cmp_w1_v': nrm((DEPTH, CMP_LEN, NSA_HEAD_DIM, CMP_HIDDEN), (CMP_LEN * NSA_HEAD_DIM) ** -0.5),
        'cmp_w2_v': nrm((DEPTH, CMP_HIDDEN, NSA_HEAD_DIM), CMP_HIDDEN ** -0.5),
        'dn_conv_w': nrm((DEPTH, CONV_W, DN_QKV), CONV_W ** -0.5),
        'dn_a_log': a_log,
        'dn_dt_bias': dt_bias,
        'dn_norm_g': 1.0 + nrm((DEPTH, DN_DV), 0.02),
        'w_out': nrm((DEPTH, D_MODEL, D_MODEL), D_MODEL ** -0.5),
        'w_up': nrm((DEPTH, D_MODEL, D_FF), D_MODEL ** -0.5),
        'w_down': nrm((DEPTH, D_FF, D_MODEL), D_FF ** -0.5),
        'final_g': 1.0 + nrm((D_MODEL,), 0.02),
    }


def reference(x_prompt, x_sample, c_prompt, c_sample, cache_k_cmp, cache_v_cmp, cache_k_slc, cache_v_slc,
              cache_k_win, cache_v_win, state_conv, state_dn, page_table, w_ada, b_ada, norm1_g, norm2_g,
              w_in, cmp_pe_k, cmp_w1_k, cmp_w2_k, cmp_pe_v, cmp_w1_v, cmp_w2_v, dn_conv_w, dn_a_log,
              dn_dt_bias, dn_norm_g, w_out, w_up, w_down, final_g):
    xp, xs = x_prompt, x_sample
    p_new = [[] for _ in range(8)]
    s_new = [[] for _ in range(8)]
    for l in range(DEPTH):
        cmp_params = (cmp_pe_k[l], cmp_w1_k[l], cmp_w2_k[l], cmp_pe_v[l], cmp_w1_v[l], cmp_w2_v[l])
        shared = (w_ada[l], b_ada[l], norm1_g[l], norm2_g[l], w_in[l], dn_conv_w[l], dn_a_log[l],
                  dn_dt_bias[l], dn_norm_g[l], w_out[l], w_up[l], w_down[l])
        conv0 = jnp.zeros((xp.shape[0], CONV_W - 1, DN_QKV), xp.dtype)
        dn0 = jnp.zeros((xp.shape[0], DN_HEADS, DN_DK, DN_DV), jnp.float32)
        prompt_nsa = functools.partial(_nsa_prompt, cmp_params=cmp_params)
        xp, nsa_p, conv_p, dn_p = _layer(xp, c_prompt, prompt_nsa, conv0, dn0, *shared)
        caches = (cache_k_cmp[l], cache_v_cmp[l], cache_k_slc[l], cache_v_slc[l], cache_k_win[l], cache_v_win[l])
        sample_nsa = functools.partial(_nsa_sample, cmp_params=cmp_params, caches=caches, page_table=page_table)
        xs, nsa_s, conv_s, dn_s = _layer(xs, c_sample, sample_nsa, state_conv[l], state_dn[l], *shared)
        for i, t in enumerate(nsa_p + (conv_p, dn_p)):
            p_new[i].append(t)
        for i, t in enumerate(nsa_s + (conv_s, dn_s)):
            s_new[i].append(t)
    y_prompt = _rmsnorm(xp, final_g)
    y_sample = _rmsnorm(xs, final_g)
    return (y_prompt, y_sample,
            jnp.stack(p_new[0]), jnp.stack(p_new[1]), jnp.stack(p_new[2]), jnp.stack(p_new[3]),
            jnp.stack(p_new[4]), jnp.stack(p_new[5]), jnp.stack(p_new[6]), jnp.stack(p_new[7]),
            jnp.stack(s_new[0]), jnp.stack(s_new[1]), jnp.stack(s_new[2]), jnp.stack(s_new[3]),
            jnp.stack(s_new[4]), jnp.stack(s_new[5]), jnp.stack(s_new[6]), jnp.stack(s_new[7]))
```

```python
import functools

import numpy as np
import jax
import jax.numpy as jnp
from jax import lax
from jax.experimental import pallas as pl
from jax.experimental.pallas import tpu as pltpu

F32 = jnp.float32
BF16 = jnp.bfloat16

D_MODEL = 1024
N_HEADS = 16
HEAD_DIM = 64
KV_HEADS = 4
GROUP = N_HEADS // KV_HEADS
KV_WIDTH = KV_HEADS * HEAD_DIM
CMP_STRIDE = 16
CMP_LEN = 32
CMP_HIDDEN = 128
SLC_BLOCK = 64
SLC_TOPN = 16
WINDOW = 512
DN_HEADS = 8
DN_DK = 128
DN_QKV = 3072
CONV_W = 4
D_FF = 4096
EPS = 1e-6
NEG = -1e30
LANE = 128
VMEM_LIMIT = 56 * 1024 * 1024

A_LANE = 3 * N_HEADS
B_LANE = A_LANE + DN_HEADS


def _cparams(sem):
    return pltpu.CompilerParams(dimension_semantics=sem, vmem_limit_bytes=VMEM_LIMIT)


def _dot(a, b):
    return jnp.dot(a.astype(BF16), b.astype(BF16), preferred_element_type=F32)


def _dot_nt(a, b):
    return lax.dot_general(a.astype(BF16), b.astype(BF16), (((1,), (1,)), ((), ())),
                           preferred_element_type=F32)


def _split3(x):
    hi = x.astype(BF16)
    r = x - hi.astype(F32)
    mid = r.astype(BF16)
    lo = (r - mid.astype(F32)).astype(BF16)
    return hi, mid, lo


def _dot_exact_lhs(a01, x):
    a = a01.astype(BF16)
    hi, mid, lo = _split3(x)
    return (jnp.dot(a, hi, preferred_element_type=F32) + jnp.dot(a, mid, preferred_element_type=F32)
            + jnp.dot(a, lo, preferred_element_type=F32))


def _dot_exact_rhs(x, b01):
    b = b01.astype(BF16)
    hi, mid, lo = _split3(x)
    return (jnp.dot(hi, b, preferred_element_type=F32) + jnp.dot(mid, b, preferred_element_type=F32)
            + jnp.dot(lo, b, preferred_element_type=F32))


def _sigmoid(x):
    return 1.0 / (1.0 + jnp.exp(-x))


def _softplus(x):
    return jnp.maximum(x, 0.0) + jnp.log(1.0 + jnp.exp(-jnp.abs(x)))


def _norm_mod(x, g, sc, sh):
    y = x * lax.rsqrt(jnp.mean(x * x, axis=-1, keepdims=True) + EPS)
    return (y * g) * (1.0 + sc) + sh


def _ada_kernel(c_ref, w_ref, b_ref, o_ref):
    o_ref[...] = _dot(c_ref[...], w_ref[...]) + b_ref[...]


def _ada(c_all, w_ada, b_ada):
    rows = c_all.shape[0]
    n_out = w_ada.shape[1]
    tn = D_MODEL
    return pl.pallas_call(
        _ada_kernel,
        grid=(n_out // tn,),
        in_specs=[pl.BlockSpec((rows, D_MODEL), lambda j: (0, 0)),
                  pl.BlockSpec((D_MODEL, tn), lambda j: (0, j)),
                  pl.BlockSpec((1, tn), lambda j: (0, j))],
        out_specs=pl.BlockSpec((rows, tn), lambda j: (0, j)),
        out_shape=jax.ShapeDtypeStruct((rows, n_out), F32),
        compiler_params=_cparams(("arbitrary",)),
        name="ada",
    )(c_all, w_ada, b_ada)


PROJ_CHUNK = 512


def _proj_kernel(kinds, x_ref, mod_ref, g_ref, *refs):
    nw = len(kinds)
    w_refs, o_refs = refs[:nw], refs[nw:]
    h = _norm_mod(x_ref[...], g_ref[...], mod_ref[1], mod_ref[0]).astype(BF16)
    for kind, w_ref, o_ref in zip(kinds, w_refs, o_refs):
        if kind == "nn":
            width = w_ref.shape[1]
            for c in range(0, width, PROJ_CHUNK):
                e = min(c + PROJ_CHUNK, width)
                o_ref[:, c:e] = jnp.dot(h, w_ref[:, c:e], preferred_element_type=F32).astype(o_ref.dtype)
        else:
            width = w_ref.shape[0]
            for c in range(0, width, PROJ_CHUNK):
                e = min(c + PROJ_CHUNK, width)
                o_ref[c:e, :] = lax.dot_general(w_ref[c:e, :], h, (((1,), (1,)), ((), ())),
                                                preferred_element_type=F32).astype(o_ref.dtype)


def _proj(x, mod, g, weights, kinds, dtypes, tm, name):
    G, R, _ = x.shape
    rm = mod.shape[2]
    grid = (G, R // tm)
    in_specs = [pl.BlockSpec((None, tm, D_MODEL), lambda n, i: (n, i, 0)),
                pl.BlockSpec((None, 6, rm, D_MODEL), lambda n, i: (n, 0, 0, 0)),
                pl.BlockSpec((1, D_MODEL), lambda n, i: (0, 0))]
    out_specs, out_shape = [], []
    for w, kind, dt in zip(weights, kinds, dtypes):
        in_specs.append(pl.BlockSpec(w.shape, lambda n, i: (0, 0)))
        if kind == "nn":
            width = w.shape[1]
            out_specs.append(pl.BlockSpec((None, tm, width), lambda n, i: (n, i, 0)))
            out_shape.append(jax.ShapeDtypeStruct((G, R, width), dt))
        else:
            width = w.shape[0]
            out_specs.append(pl.BlockSpec((None, width, tm), lambda n, i: (n, 0, i)))
            out_shape.append(jax.ShapeDtypeStruct((G, width, R), dt))
    return pl.pallas_call(
        functools.partial(_proj_kernel, tuple(kinds)),
        grid=grid, in_specs=in_specs, out_specs=out_specs, out_shape=out_shape,
        compiler_params=_cparams(("parallel", "parallel")),
        name=name,
    )(x, mod, g, *weights)


def _compress_core(full_ref, kp_ref, wa_ref, wb_ref, pe2_ref, w1f_ref, w2t_ref, length):
    n_chunks = length // CMP_STRIDE
    pe_term = _dot(pe2_ref[...], w1f_ref[...])[0:1]
    outs = []
    for pair in range(KV_HEADS // 2):
        kp_ref[pair, 0:length, :] = full_ref[pair * LANE:(pair + 1) * LANE, :].T
        kp_ref[pair, length:length + CMP_STRIDE, :] = jnp.zeros((CMP_STRIDE, LANE), F32)
        acc = jnp.zeros((n_chunks, 2 * CMP_HIDDEN), F32)
        for p in range(CMP_STRIDE):
            a = kp_ref[pair, pl.ds(p, n_chunks, stride=CMP_STRIDE), :]
            b = kp_ref[pair, pl.ds(CMP_STRIDE + p, n_chunks, stride=CMP_STRIDE), :]
            acc = acc + _dot(a, wa_ref[p]) + _dot(b, wb_ref[p])
        for gg in range(2):
            hid = jax.nn.gelu(acc[:, gg * CMP_HIDDEN:(gg + 1) * CMP_HIDDEN] + pe_term)
            outs.append(_dot_nt(w2t_ref[...], hid))
    return outs


def _compress_kernel(length, kt_ref, wa_ref, wb_ref, pe2_ref, w1f_ref, w2t_ref, o_ref, kp_ref):
    outs = _compress_core(kt_ref, kp_ref, wa_ref, wb_ref, pe2_ref, w1f_ref, w2t_ref, length)
    for g in range(KV_HEADS):
        o_ref[g] = outs[g]


def _compress_prompt(kt, cw):
    n, _, length = kt.shape
    n_chunks = length // CMP_STRIDE
    const = lambda a: pl.BlockSpec(a.shape, lambda i: (0,) * a.ndim)
    return pl.pallas_call(
        functools.partial(_compress_kernel, length),
        grid=(n,),
        in_specs=[pl.BlockSpec((None, KV_WIDTH, length), lambda i: (i, 0, 0))] + [const(a) for a in cw],
        out_specs=pl.BlockSpec((None, KV_HEADS, HEAD_DIM, n_chunks), lambda i: (i, 0, 0, 0)),
        out_shape=jax.ShapeDtypeStruct((n, KV_HEADS, HEAD_DIM, n_chunks), F32),
        scratch_shapes=[pltpu.VMEM((KV_HEADS // 2, length + CMP_STRIDE, LANE), F32)],
        compiler_params=_cparams(("parallel",)),
        name="compress_prompt",
    )(kt, *cw)


def _compress_weights(pe, w1, w2):
    eye2 = jnp.eye(2, dtype=F32)
    wa = jnp.einsum("pde,ab->padbe", w1[:CMP_STRIDE], eye2).reshape(CMP_STRIDE, LANE, 2 * CMP_HIDDEN)
    wb = jnp.einsum("pde,ab->padbe", w1[CMP_STRIDE:], eye2).reshape(CMP_STRIDE, LANE, 2 * CMP_HIDDEN)
    pe2 = jnp.zeros((8, CMP_LEN * HEAD_DIM), F32).at[0].set(pe.reshape(-1))
    w1f = w1.reshape(CMP_LEN * HEAD_DIM, CMP_HIDDEN)
    return (wa.astype(BF16), wb.astype(BF16), pe2, w1f.astype(BF16), w2.T.astype(BF16))


def _overlap_matrix(n_cmp, n_blk, rows, cols):
    c0 = np.arange(n_cmp)[:, None] * CMP_STRIDE
    b0 = np.arange(n_blk)[None, :] * SLC_BLOCK
    inter = np.minimum(c0 + CMP_LEN, b0 + SLC_BLOCK) - np.maximum(c0, b0)
    ov = np.zeros((rows, cols), np.float32)
    ov[:n_cmp, :n_blk] = np.clip(inter, 0, None) / CMP_LEN
    return jnp.asarray(ov)


def _topk_select(imp, n_blk, top_n):
    lane = lax.broadcasted_iota(jnp.int32, imp.shape, 1)
    rank = jnp.zeros(imp.shape, F32)
    for i in range(n_blk):
        col = imp[:, i:i + 1]
        ahead = (col > imp) | ((col == imp) & (lane > i))
        rank = rank + ahead.astype(F32)
    return (rank < float(top_n)).astype(F32)


def _imp_matmul(psum, ov):
    hi = psum.astype(BF16)
    lo = (psum - hi.astype(F32)).astype(BF16)
    ovb = ov.astype(BF16)
    return jnp.dot(hi, ovb, preferred_element_type=F32) + jnp.dot(lo, ovb, preferred_element_type=F32)


NSA_TQ = 128
NSA_TK = 256


def _nsa_prompt_kernel(seq, q_ref, gate_ref, kc_ref, vc_ref, ks_ref, vs_ref, kw_ref, vw_ref, ov_ref, o_ref):
    tq, tk = NSA_TQ, NSA_TK
    g = pl.program_id(1)
    i = pl.program_id(2)
    s0 = i * tq
    n_cmp = seq // CMP_STRIDE - 1
    n_blk = seq // SLC_BLOCK
    top_n = min(SLC_TOPN, n_blk)

    q = q_ref[...].astype(F32) * (HEAD_DIM ** -0.5)
    q4 = jnp.concatenate([q[:, h * HEAD_DIM:(h + 1) * HEAD_DIM] for h in range(GROUP)], axis=0).astype(BF16)
    qpos = s0 + lax.broadcasted_iota(jnp.int32, (tq, 1), 0)

    n_c = kc_ref.shape[1]
    s = _dot(q4, kc_ref[...]).reshape(GROUP, tq, n_c)
    cidx = lax.broadcasted_iota(jnp.int32, (tq, n_c), 1)
    valid = ((cidx * CMP_STRIDE + (CMP_LEN - 1)) <= qpos) & (cidx < n_cmp)
    sm = jnp.where(valid[None], s, NEG)
    m = jnp.max(sm, axis=-1, keepdims=True)
    p = jnp.where(valid[None], jnp.exp(sm - m), 0.0)
    p = p / jnp.maximum(jnp.sum(p, axis=-1, keepdims=True), 1e-30)
    o_cmp = _dot_nt(p.reshape(GROUP * tq, n_c), vc_ref[...])
    psum = p[0] + p[1] + p[2] + p[3]
    imp = _imp_matmul(psum, ov_ref[...])

    blk = lax.broadcasted_iota(jnp.int32, (tq, LANE), 1)
    forced = (blk == qpos // SLC_BLOCK) | (blk == 0)
    causal = (blk * SLC_BLOCK) <= qpos
    imp = jnp.where(forced, jnp.inf, jnp.where(causal, imp, -jnp.inf))
    sel = _topk_select(imp, n_blk, top_n).astype(BF16)

    def flash(k_ref, v_ref, kt_lo, kt_hi, mask_fn):
        def body(kt, carry):
            m, l, acc = carry
            k0 = pl.multiple_of(kt * tk, tk)
            kT = k_ref[:, pl.ds(k0, tk)]
            vT = v_ref[:, pl.ds(k0, tk)]
            s = _dot(q4, kT).reshape(GROUP, tq, tk)
            kpos = k0 + lax.broadcasted_iota(jnp.int32, (tq, tk), 1)
            ok = mask_fn(kt, kpos)[None]
            sm = jnp.where(ok, s, NEG)
            m_new = jnp.maximum(m, jnp.max(sm, axis=-1, keepdims=True))
            alpha = jnp.exp(m - m_new)
            p = jnp.where(ok, jnp.exp(sm - m_new), 0.0)
            l = alpha * l + jnp.sum(p, axis=-1, keepdims=True)
            pv = _dot_nt(p.reshape(GROUP * tq, tk), vT)
            acc = alpha.reshape(GROUP * tq, 1) * acc + pv
            return m_new, l, acc
        init = (jnp.full((GROUP, tq, 1), NEG, F32), jnp.zeros((GROUP, tq, 1), F32),
                jnp.zeros((GROUP * tq, HEAD_DIM), F32))
        m, l, acc = lax.fori_loop(kt_lo, kt_hi, body, init)
        return acc / jnp.maximum(l.reshape(GROUP * tq, 1), 1e-30)

    kt_hi = (s0 + tq - 1) // tk + 1
    blk_per_tile = tk // SLC_BLOCK

    def slc_mask(kt, kpos):
        row = lax.broadcasted_iota(jnp.int32, (LANE, tk), 0)
        col = lax.broadcasted_iota(jnp.int32, (LANE, tk), 1)
        expand = (row == kt * blk_per_tile + col // SLC_BLOCK).astype(BF16)
        chosen = jnp.dot(sel, expand, preferred_element_type=F32) > 0.5
        return chosen & (kpos <= qpos)

    def win_mask(kt, kpos):
        return (kpos <= qpos) & (kpos > qpos - WINDOW)

    o_slc = flash(ks_ref, vs_ref, 0, kt_hi, slc_mask)
    kt_lo = jnp.maximum(s0 - (WINDOW - 1), 0) // tk
    o_win = flash(kw_ref, vw_ref, kt_lo, kt_hi, win_mask)

    sg = _sigmoid(gate_ref[...])
    lane = lax.broadcasted_iota(jnp.int32, (tq, LANE), 1)
    for h in range(GROUP):
        base = g * (GROUP * 3) + h * 3
        gc = [jnp.sum(jnp.where(lane == base + b, sg, 0.0), axis=-1, keepdims=True) for b in range(3)]
        rows = slice(h * tq, (h + 1) * tq)
        out = gc[0] * o_cmp[rows] + gc[1] * o_slc[rows] + gc[2] * o_win[rows]
        o_ref[:, h * HEAD_DIM:(h + 1) * HEAD_DIM] = out


def _nsa_prompt(q, small, kct, vct, kst, vst, kwt, vwt):
    n, seq, _ = q.shape
    n_c = kct.shape[-1]
    ov = _overlap_matrix(seq // CMP_STRIDE - 1, seq // SLC_BLOCK, n_c, LANE)
    cw = GROUP * HEAD_DIM
    head_spec = lambda width: pl.BlockSpec((None, None, HEAD_DIM, width), lambda b, g, i: (b, g, 0, 0))
    r4 = lambda a: a.reshape(n, KV_HEADS, HEAD_DIM, seq)
    return pl.pallas_call(
        functools.partial(_nsa_prompt_kernel, seq),
        grid=(n, KV_HEADS, seq // NSA_TQ),
        in_specs=[pl.BlockSpec((None, NSA_TQ, cw), lambda b, g, i: (b, i, g)),
                  pl.BlockSpec((None, NSA_TQ, LANE), lambda b, g, i: (b, i, 0)),
                  head_spec(n_c), head_spec(n_c), head_spec(seq), head_spec(seq), head_spec(seq), head_spec(seq),
                  pl.BlockSpec(ov.shape, lambda b, g, i: (0, 0))],
        out_specs=pl.BlockSpec((None, NSA_TQ, cw), lambda b, g, i: (b, i, g)),
        out_shape=jax.ShapeDtypeStruct((n, seq, D_MODEL), F32),
        compiler_params=_cparams(("parallel", "parallel", "parallel")),
        name="nsa_prompt",
    )(q, small, kct, vct, r4(kst), r4(vst), r4(kwt), r4(vwt), ov)


DN_C = 128


def _dn_gates(small, alog_ref, dt_ref):
    g = -jnp.exp(alog_ref[...]) * _softplus(small + dt_ref[...])
    return g, _sigmoid(small)


def _l2n(x):
    return x * lax.rsqrt(jnp.sum(x * x, axis=-1, keepdims=True) + EPS)


def _dn_out(o, z, ng):
    on = o * lax.rsqrt(jnp.mean(o * o, axis=-1, keepdims=True) + EPS) * ng
    return on * (z * _sigmoid(z))


INV_BASE = 16


def _unit_lower_inverse(lmat, eye, ri, ci, c):
    same = (ri // INV_BASE) == (ci // INV_BASE)
    pw = jnp.where(same, -lmat, 0.0)
    x = eye + pw
    for _ in range(int(np.log2(INV_BASE)) - 1):
        pw = _dot(pw, pw)
        x = x + _dot(x, pw)
    s = INV_BASE
    while s < c:
        off = ((ri // (2 * s)) == (ci // (2 * s))) & ((ri // s) != (ci // s))
        x = x - _dot(_dot(x, jnp.where(off, lmat, 0.0)), x)
        s *= 2
    return x


def _dn_prompt_kernel(qkv_ref, z_ref, small_ref, cw_ref, alog_ref, dt_ref, ng_ref, o_ref, st_ref, xbuf, s_ref):
    c = DN_C
    i = pl.program_id(1)
    last = pl.num_programs(1) - 1

    @pl.when(i == 0)
    def _():
        xbuf[0:8, :] = jnp.zeros((8, DN_QKV), F32)
        s_ref[...] = jnp.zeros_like(s_ref)

    xbuf[8:8 + c, :] = qkv_ref[...]
    y = (cw_ref[3:4, :] * xbuf[8:8 + c, :] + cw_ref[2:3, :] * xbuf[7:7 + c, :]
         + cw_ref[1:2, :] * xbuf[6:6 + c, :] + cw_ref[0:1, :] * xbuf[5:5 + c, :])
    xbuf[0:8, :] = xbuf[c:c + 8, :]
    y = y * _sigmoid(y)

    gl, bt = _dn_gates(small_ref[...], alog_ref, dt_ref)
    ri = lax.broadcasted_iota(jnp.int32, (c, c), 0)
    ci = lax.broadcasted_iota(jnp.int32, (c, c), 1)
    incl = ri >= ci
    strict = ri > ci
    gc_col = _dot_exact_lhs(incl.astype(F32), gl)
    gc_row = _dot_exact_rhs(gl.T, (ri <= ci).astype(F32))
    eye = (ri == ci).astype(F32)
    nd = DN_HEADS * DN_DK

    for h in range(DN_HEADS):
        sl = slice(h * DN_DK, (h + 1) * DN_DK)
        q = _l2n(y[:, sl]) * (DN_DK ** -0.5)
        k = _l2n(y[:, nd + h * DN_DK:nd + (h + 1) * DN_DK])
        v = y[:, 2 * nd + h * DN_DK:2 * nd + (h + 1) * DN_DK]
        gcc = gc_col[:, A_LANE + h:A_LANE + h + 1]
        gcr = gc_row[A_LANE + h:A_LANE + h + 1, :]
        beta = bt[:, B_LANE + h:B_LANE + h + 1]
        decay = jnp.exp(jnp.where(incl, gcc - gcr, NEG))
        kb = k * beta
        lmat = jnp.where(strict, _dot_nt(kb, k) * decay, 0.0)
        tmat = _unit_lower_inverse(lmat, eye, ri, ci, c)
        eg = jnp.exp(gcc)
        u = _dot(tmat, v * beta)
        w = _dot(tmat, kb * eg)
        a_intra = _dot_nt(q, k) * decay
        g_last = gcr[:, c - 1:c]
        q_dec = q * eg
        k_dec = k * jnp.exp(g_last - gcc)
        state = s_ref[h]
        v_new = u - _dot(w, state)
        o = _dot(q_dec, state) + _dot(a_intra, v_new)
        s_ref[h] = state * jnp.exp(g_last) + _dot(k_dec.T, v_new)
        o_ref[:, sl] = _dn_out(o, z_ref[:, sl], ng_ref[...])

    @pl.when(i == last)
    def _():
        st_ref[...] = s_ref[...]


def _dn_prompt(qkv, z, small, conv_w, alog_row, dt_row, norm_g):
    n, seq, _ = qkv.shape
    c = DN_C
    const = lambda a: pl.BlockSpec(a.shape, lambda b, i: (0,) * a.ndim)
    return pl.pallas_call(
        _dn_prompt_kernel,
        grid=(n, seq // c),
        in_specs=[pl.BlockSpec((None, c, DN_QKV), lambda b, i: (b, i, 0)),
                  pl.BlockSpec((None, c, D_MODEL), lambda b, i: (b, i, 0)),
                  pl.BlockSpec((None, c, LANE), lambda b, i: (b, i, 0)),
                  const(conv_w), const(alog_row), const(dt_row), const(norm_g)],
        out_specs=[pl.BlockSpec((None, c, D_MODEL), lambda b, i: (b, i, 0)),
                   pl.BlockSpec((None, DN_HEADS, DN_DK, DN_DK), lambda b, i: (b, 0, 0, 0))],
        out_shape=[jax.ShapeDtypeStruct((n, seq, D_MODEL), F32),
                   jax.ShapeDtypeStruct((n, DN_HEADS, DN_DK, DN_DK), F32)],
        scratch_shapes=[pltpu.VMEM((c + 8, DN_QKV), F32), pltpu.VMEM((DN_HEADS, DN_DK, DN_DK), F32)],
        compiler_params=_cparams(("parallel", "arbitrary")),
        name="dn_prompt",
    )(qkv, z, small, conv_w, alog_row, dt_row, norm_g)


MLP_TF = 1024


def _mlp_kernel(x_ref, on_ref, od_ref, mg_ref, mod_ref, g2_ref, gf_ref, wo_ref, wu_ref, wd_ref, y_ref,
                x1_ref, h2_ref, acc_ref):
    j = pl.program_id(2)
    last = pl.num_programs(2) - 1

    @pl.when(j == 0)
    def _():
        ga = _sigmoid(mg_ref[:, :D_MODEL])
        gb = _sigmoid(mg_ref[:, D_MODEL:])
        mixed = ga * on_ref[...] + gb * od_ref[...]
        x1 = x_ref[...] + mod_ref[2] * _dot(mixed, wo_ref[...])
        x1_ref[...] = x1
        h2_ref[...] = _norm_mod(x1, g2_ref[...], mod_ref[4], mod_ref[3]).astype(BF16)
        acc_ref[...] = jnp.zeros_like(acc_ref)

    up = jnp.dot(h2_ref[...], wu_ref[...], preferred_element_type=F32)
    act = jnp.square(jnp.maximum(up, 0.0))
    acc_ref[...] += _dot(act, wd_ref[...])

    @pl.when(j == last)
    def _():
        x2 = x1_ref[...] + mod_ref[5] * acc_ref[...]
        y_ref[...] = x2 * lax.rsqrt(jnp.mean(x2 * x2, axis=-1, keepdims=True) + EPS) * gf_ref[...]


def _mlp(x, o_nsa, o_dn, merge, mod, g2, gf, w_out, w_up, w_down, tm, name):
    G, R, _ = x.shape
    rm = mod.shape[2]
    tf = MLP_TF
    row = lambda width: pl.BlockSpec((None, tm, width), lambda n, i, j: (n, i, 0))
    return pl.pallas_call(
        _mlp_kernel,
        grid=(G, R // tm, D_FF // tf),
        in_specs=[row(D_MODEL), row(D_MODEL), row(D_MODEL), row(2 * D_MODEL),
                  pl.BlockSpec((None, 6, rm, D_MODEL), lambda n, i, j: (n, 0, 0, 0)),
                  pl.BlockSpec((1, D_MODEL), lambda n, i, j: (0, 0)),
                  pl.BlockSpec((1, D_MODEL), lambda n, i, j: (0, 0)),
                  pl.BlockSpec((D_MODEL, D_MODEL), lambda n, i, j: (0, 0)),
                  pl.BlockSpec((D_MODEL, tf), lambda n, i, j: (0, j)),
                  pl.BlockSpec((tf, D_MODEL), lambda n, i, j: (j, 0))],
        out_specs=row(D_MODEL),
        out_shape=jax.ShapeDtypeStruct((G, R, D_MODEL), F32),
        scratch_shapes=[pltpu.VMEM((tm, D_MODEL), F32), pltpu.VMEM((tm, D_MODEL), BF16),
                        pltpu.VMEM((tm, D_MODEL), F32)],
        compiler_params=_cparams(("parallel", "parallel", "arbitrary")),
        name=name,
    )(x, o_nsa, o_dn, merge, mod, g2, gf, w_out, w_up, w_down)


def _nsa_sample_kernel(n_pages, page, win, *refs):
    n_in = 4 * n_pages
    pt_ref = refs[0]
    del pt_ref
    pools = [refs[1 + t * n_pages:1 + (t + 1) * n_pages] for t in range(4)]
    (q_ref, gate_ref, new_ref, kwc_ref, vwc_ref, ov_ref, ex_ref,
     wak, wbk, pe2k, w1fk, w2tk, wav, wbv, pe2v, w1fv, w2tv) = refs[1 + n_in:1 + n_in + 17]
    o_ref = refs[1 + n_in + 17]
    full_a, full_b, kp_ref, winbuf_k, winbuf_v = refs[1 + n_in + 18:]

    past = n_pages * page
    total = past + 1
    lp = -(-total // SLC_BLOCK) * SLC_BLOCK
    n_cmp = lp // CMP_STRIDE - 1
    n_blk = lp // SLC_BLOCK
    top_n = min(SLC_TOPN, n_blk)
    length = full_a.shape[1]
    n_c = length // CMP_STRIDE

    def fill(full_ref, page_refs, t):
        for j in range(n_pages):
            full_ref[:, j * page:(j + 1) * page] = page_refs[j][...]
        full_ref[:, past:length] = jnp.zeros((KV_WIDTH, length - past), F32)
        full_ref[:, past:past + 1] = new_ref[t]

    q16 = (q_ref[...].astype(F32) * (HEAD_DIM ** -0.5)).astype(BF16)
    head_group = lax.broadcasted_iota(jnp.int32, (N_HEADS, 1), 0) // GROUP

    def by_group(vals):
        out = vals[KV_HEADS - 1]
        for g in range(KV_HEADS - 2, -1, -1):
            out = jnp.where(head_group == g, vals[g], out)
        return out

    def attend(kts, vts, valid):
        s = by_group([_dot(q16, kt) for kt in kts])
        sm = jnp.where(valid, s, NEG)
        m = jnp.max(sm, axis=-1, keepdims=True)
        p = jnp.where(valid, jnp.exp(sm - m), 0.0)
        p = p / jnp.maximum(jnp.sum(p, axis=-1, keepdims=True), 1e-30)
        o = by_group([_dot_nt(p, vt) for vt in vts])
        return p, o

    fill(full_a, pools[0], 0)
    kcs = _compress_core(full_a, kp_ref, wak, wbk, pe2k, w1fk, w2tk, length)
    fill(full_a, pools[1], 1)
    vcs = _compress_core(full_a, kp_ref, wav, wbv, pe2v, w1fv, w2tv, length)
    cidx = lax.broadcasted_iota(jnp.int32, (1, n_c), 1)
    valid_c = ((cidx * CMP_STRIDE + (CMP_LEN - 1)) <= past) & (cidx < n_cmp)
    p_cmp, o_cmp = attend(kcs, vcs, valid_c)
    imp_h = _imp_matmul(p_cmp, ov_ref[...])
    hr = lax.broadcasted_iota(jnp.int32, (N_HEADS, N_HEADS), 0) // GROUP
    hc = lax.broadcasted_iota(jnp.int32, (N_HEADS, N_HEADS), 1) // GROUP
    imp = _dot_exact_lhs((hr == hc).astype(F32), imp_h)
    blk = lax.broadcasted_iota(jnp.int32, (N_HEADS, LANE), 1)
    forced = (blk == past // SLC_BLOCK) | (blk == 0)
    causal = (blk * SLC_BLOCK) <= past
    imp = jnp.where(forced, jnp.inf, jnp.where(causal & (blk < n_blk), imp, -jnp.inf))
    sel = _topk_select(imp, n_blk, top_n).astype(BF16)

    fill(full_a, pools[2], 2)
    fill(full_b, pools[3], 3)
    kpos = lax.broadcasted_iota(jnp.int32, (1, length), 1)
    chosen = jnp.dot(sel, ex_ref[...], preferred_element_type=F32) > 0.5
    valid_s = chosen & (kpos <= past)
    kts = [full_a[g * HEAD_DIM:(g + 1) * HEAD_DIM, :] for g in range(KV_HEADS)]
    vts = [full_b[g * HEAD_DIM:(g + 1) * HEAD_DIM, :] for g in range(KV_HEADS)]
    _, o_slc = attend(kts, vts, valid_s)

    wl = winbuf_k.shape[1]
    for buf, cache, t in ((winbuf_k, kwc_ref, 4), (winbuf_v, vwc_ref, 5)):
        buf[:, 0:win] = cache[...]
        buf[:, win:wl] = jnp.zeros((KV_WIDTH, wl - win), F32)
        buf[:, win:win + 1] = new_ref[t]
    widx = lax.broadcasted_iota(jnp.int32, (1, wl), 1)
    wpos = past - win + widx
    valid_w = (wpos <= past) & (wpos > past - WINDOW) & (wpos >= 0)
    kts = [winbuf_k[g * HEAD_DIM:(g + 1) * HEAD_DIM, :] for g in range(KV_HEADS)]
    vts = [winbuf_v[g * HEAD_DIM:(g + 1) * HEAD_DIM, :] for g in range(KV_HEADS)]
    _, o_win = attend(kts, vts, valid_w)

    sg = _sigmoid(gate_ref[...])
    o_ref[...] = sg[:, 0:1] * o_cmp + sg[:, 1:2] * o_slc + sg[:, 2:3] * o_win


def _nsa_sample(page_table, pools, q16, gates, newcols, kwc, vwc, cwk, cwv):
    n_seq, n_pages = page_table.shape
    page = pools[0].shape[-1]
    win = kwc.shape[-1]
    past = n_pages * page
    lp = -(-(past + 1) // SLC_BLOCK) * SLC_BLOCK
    length = -(-lp // LANE) * LANE
    n_c = length // CMP_STRIDE
    ov = _overlap_matrix(lp // CMP_STRIDE - 1, lp // SLC_BLOCK, n_c, LANE)
    ex = np.zeros((LANE, length), np.float32)
    ex[np.arange(length) // SLC_BLOCK, np.arange(length)] = 1.0
    ex = jnp.asarray(ex, dtype=BF16)
    wl = win + LANE

    in_specs = []
    for _ in range(4):
        for j in range(n_pages):
            in_specs.append(pl.BlockSpec((None, KV_WIDTH, page), functools.partial(
                lambda s, pt, jj: (pt[s, jj], 0, 0), jj=j)))
    const = lambda a: pl.BlockSpec(a.shape, lambda s, pt: (0,) * a.ndim)
    in_specs += [pl.BlockSpec((None, N_HEADS, HEAD_DIM), lambda s, pt: (s, 0, 0)),
                 pl.BlockSpec((None, N_HEADS, 3), lambda s, pt: (s, 0, 0)),
                 pl.BlockSpec((None, 6, KV_WIDTH, 1), lambda s, pt: (s, 0, 0, 0)),
                 pl.BlockSpec((None, KV_WIDTH, win), lambda s, pt: (s, 0, 0)),
                 pl.BlockSpec((None, KV_WIDTH, win), lambda s, pt: (s, 0, 0)),
                 const(ov), const(ex)] + [const(a) for a in cwk] + [const(a) for a in cwv]
    operands = []
    for t in range(4):
        operands += [pools[t]] * n_pages
    operands += [q16, gates, newcols, kwc, vwc, ov, ex, *cwk, *cwv]
    grid_spec = pltpu.PrefetchScalarGridSpec(
        num_scalar_prefetch=1, grid=(n_seq,), in_specs=in_specs,
        out_specs=pl.BlockSpec((None, N_HEADS, HEAD_DIM), lambda s, pt: (s, 0, 0)),
        scratch_shapes=[pltpu.VMEM((KV_WIDTH, length), F32), pltpu.VMEM((KV_WIDTH, length), F32),
                        pltpu.VMEM((KV_HEADS // 2, length + CMP_STRIDE, LANE), F32),
                        pltpu.VMEM((KV_WIDTH, wl), F32), pltpu.VMEM((KV_WIDTH, wl), F32)])
    return pl.pallas_call(
        functools.partial(_nsa_sample_kernel, n_pages, page, win),
        grid_spec=grid_spec,
        out_shape=jax.ShapeDtypeStruct((n_seq, N_HEADS, HEAD_DIM), F32),
        compiler_params=_cparams(("arbitrary",)),
        name="nsa_sample",
    )(page_table, *operands)


DN_SB = 8


def _dn_sample_kernel(qkv_ref, conv_ref, z_ref, small_ref, st_ref, cw_ref, alog_ref, dt_ref, ng_ref,
                      o_ref, conv_o_ref, st_o_ref):
    x = qkv_ref[...]
    y = (cw_ref[0:1, :] * conv_ref[0] + cw_ref[1:2, :] * conv_ref[1] + cw_ref[2:3, :] * conv_ref[2]
         + cw_ref[3:4, :] * x)
    conv_o_ref[0] = conv_ref[1]
    conv_o_ref[1] = conv_ref[2]
    conv_o_ref[2] = x
    y = y * _sigmoid(y)
    gl, bt = _dn_gates(small_ref[...], alog_ref, dt_ref)
    nd = DN_HEADS * DN_DK
    ri = lax.broadcasted_iota(jnp.int32, (DN_DK, DN_DK), 0)
    ci = lax.broadcasted_iota(jnp.int32, (DN_DK, DN_DK), 1)
    eye = ri == ci
    for h in range(DN_HEADS):
        sl = slice(h * DN_DK, (h + 1) * DN_DK)
        q = _l2n(y[:, sl]) * (DN_DK ** -0.5)
        k = _l2n(y[:, nd + h * DN_DK:nd + (h + 1) * DN_DK])
        v = y[:, 2 * nd + h * DN_DK:2 * nd + (h + 1) * DN_DK]
        g = gl[:, A_LANE + h:A_LANE + h + 1]
        beta = bt[:, B_LANE + h:B_LANE + h + 1]
        eg = jnp.exp(g)
        u = v * beta
        w = k * beta * eg
        a_intra = jnp.sum(q * k, axis=-1, keepdims=True)
        q_dec = q * eg
        z = z_ref[:, sl]
        for b in range(DN_SB):
            state = st_ref[b, h]
            lhs = jnp.concatenate([w[b:b + 1], q_dec[b:b + 1], jnp.zeros((6, DN_DK), F32)], axis=0)
            res = _dot(lhs, state)
            v_new = u[b:b + 1] - res[0:1]
            o = res[1:2] + a_intra[b:b + 1] * v_new
            k_col = jnp.sum(jnp.where(eye, k[b:b + 1], 0.0), axis=-1, keepdims=True)
            st_o_ref[b, h] = state * eg[b:b + 1] + k_col * v_new
            o_ref[b:b + 1, sl] = _dn_out(o, z[b:b + 1], ng_ref[...])


def _dn_sample(qkv, conv_t, z, small, state, conv_w, alog_row, dt_row, norm_g):
    n_seq = qkv.shape[0]
    sb = DN_SB
    const = lambda a: pl.BlockSpec(a.shape, lambda i: (0,) * a.ndim)
    return pl.pallas_call(
        _dn_sample_kernel,
        grid=(n_seq // sb,),
        in_specs=[pl.BlockSpec((sb, DN_QKV), lambda i: (i, 0)),
                  pl.BlockSpec((CONV_W - 1, sb, DN_QKV), lambda i: (0, i, 0)),
                  pl.BlockSpec((sb, D_MODEL), lambda i: (i, 0)),
                  pl.BlockSpec((sb, LANE), lambda i: (i, 0)),
                  pl.BlockSpec((sb, DN_HEADS, DN_DK, DN_DK), lambda i: (i, 0, 0, 0)),
                  const(conv_w), const(alog_row), const(dt_row), const(norm_g)],
        out_specs=[pl.BlockSpec((sb, D_MODEL), lambda i: (i, 0)),
                   pl.BlockSpec((CONV_W - 1, sb, DN_QKV), lambda i: (0, i, 0)),
                   pl.BlockSpec((sb, DN_HEADS, DN_DK, DN_DK), lambda i: (i, 0, 0, 0))],
        out_shape=[jax.ShapeDtypeStruct((n_seq, D_MODEL), F32),
                   jax.ShapeDtypeStruct((CONV_W - 1, n_seq, DN_QKV), F32),
                   jax.ShapeDtypeStruct(state.shape, F32)],
        compiler_params=_cparams(("parallel",)),
        name="dn_sample",
    )(qkv, conv_t, z, small, state, conv_w, alog_row, dt_row, norm_g)


def _pick_tile(rows, pref):
    t = min(rows, pref)
    while rows % t:
        t //= 2
    return t


def _layer_weights(w_in):
    w = w_in.astype(BF16)
    o = 0
    wq = w[:, o:o + D_MODEL]; o += D_MODEL
    wkv = [w[:, o + t * KV_WIDTH:o + (t + 1) * KV_WIDTH].T for t in range(6)]; o += 6 * KV_WIDTH
    wg = w[:, o:o + 3 * N_HEADS]; o += 3 * N_HEADS
    wqkv = w[:, o:o + DN_QKV]; o += DN_QKV
    wz = w[:, o:o + D_MODEL]; o += D_MODEL
    wa = w[:, o:o + DN_HEADS]; o += DN_HEADS
    wb = w[:, o:o + DN_HEADS]; o += DN_HEADS
    wm = w[:, o:o + 2 * D_MODEL]
    wsmall = jnp.concatenate([wg, wa, wb, jnp.zeros((D_MODEL, LANE - B_LANE - DN_HEADS), BF16)], axis=1)
    return wq, wkv, wqkv, wz, wm, wsmall


def kernel(x_prompt, x_sample, c_prompt, c_sample, cache_k_cmp, cache_v_cmp, cache_k_slc, cache_v_slc, cache_k_win, cache_v_win, state_conv, state_dn, page_table, w_ada, b_ada, norm1_g, norm2_g, w_in, cmp_pe_k, cmp_w1_k, cmp_w2_k, cmp_pe_v, cmp_w1_v, cmp_w2_v, dn_conv_w, dn_a_log, dn_dt_bias, dn_norm_g, w_out, w_up, w_down, final_g):
    assert w_ada.shape[0] == 1, "single-layer stack"
    nb, seq, _ = x_prompt.shape
    ns = x_sample.shape[0]
    assert x_sample.shape[1] == 1

    n_c = nb + ns
    rows = -(-n_c // 8) * 8
    c_all = jnp.concatenate([c_prompt, c_sample, jnp.zeros((rows - n_c, D_MODEL), F32)], axis=0)
    mod = _ada(c_all, w_ada[0], b_ada)
    mod_p = mod[:nb].reshape(nb, 6, 1, D_MODEL)
    mod_s = mod[nb:n_c].reshape(ns, 6, D_MODEL).transpose(1, 0, 2)[None]

    wq, wkv, wqkv, wz, wm, wsmall = _layer_weights(w_in[0])
    g1 = norm1_g
    g2 = norm2_g
    gf = final_g.reshape(1, D_MODEL)
    alog_row = jnp.zeros((1, LANE), F32).at[0, A_LANE:A_LANE + DN_HEADS].set(dn_a_log[0])
    dt_row = jnp.zeros((1, LANE), F32).at[0, A_LANE:A_LANE + DN_HEADS].set(dn_dt_bias[0])
    conv_w = dn_conv_w[0]
    cwk = _compress_weights(cmp_pe_k[0], cmp_w1_k[0], cmp_w2_k[0])
    cwv = _compress_weights(cmp_pe_v[0], cmp_w1_v[0], cmp_w2_v[0])
    wo = w_out[0].astype(BF16)
    wu = w_up[0].astype(BF16)
    wd = w_down[0].astype(BF16)

    def project(x, modg, tm, tag):
        q, *kvt = _proj(x, modg, g1, [wq] + wkv, ["nn"] + ["nt"] * 6, [BF16] + [F32] * 6, tm, "proj_a_" + tag)
        (qkv,) = _proj(x, modg, g1, [wqkv], ["nn"], [F32], tm, "proj_b_" + tag)
        z, merge, small = _proj(x, modg, g1, [wz, wm, wsmall], ["nn"] * 3, [F32] * 3, tm, "proj_c_" + tag)
        return q, kvt, qkv, z, merge, small

    def kv_out(a):
        n, _, length = a.shape
        return a.reshape(n, KV_HEADS, HEAD_DIM, length).transpose(0, 3, 1, 2)[None]

    tm_p = _pick_tile(seq, 512)
    q, kvt, qkv, z, merge, small = project(x_prompt, mod_p, tm_p, "p")
    kct = _compress_prompt(kvt[0], cwk)
    vct = _compress_prompt(kvt[1], cwv)
    o_nsa = _nsa_prompt(q, small, kct, vct, kvt[2], kvt[3], kvt[4], kvt[5])
    o_dn, p_dn = _dn_prompt(qkv, z, small, conv_w, alog_row, dt_row, dn_norm_g)
    y_prompt = _mlp(x_prompt, o_nsa, o_dn, merge, mod_p, g2, gf, wo, wu, wd, tm_p, "mlp_p")
    wlen = min(WINDOW, seq)
    p_conv = qkv[:, seq - (CONV_W - 1):, :]

    xs = x_sample.reshape(1, ns, D_MODEL)
    qs, kvts, qkvs, zs, merges, smalls = project(xs, mod_s, ns, "s")
    pools = [c[0].transpose(0, 2, 3, 1).reshape(c.shape[1], KV_WIDTH, c.shape[2])
             for c in (cache_k_cmp, cache_v_cmp, cache_k_slc, cache_v_slc)]
    wins = [c[0].transpose(0, 2, 3, 1).reshape(ns, KV_WIDTH, c.shape[2]) for c in (cache_k_win, cache_v_win)]
    newcols = jnp.stack([a[0].T for a in kvts], axis=1)[..., None]
    q16 = qs.reshape(ns, N_HEADS, HEAD_DIM)
    gates = smalls[0, :, :3 * N_HEADS].reshape(ns, N_HEADS, 3)
    o_nsa_s = _nsa_sample(page_table, pools, q16, gates, newcols, wins[0], wins[1], cwk, cwv)
    conv_t = state_conv[0].transpose(1, 0, 2)
    o_dn_s, conv_new, s_dn = _dn_sample(qkvs[0], conv_t, zs[0], smalls[0], state_dn[0], conv_w, alog_row,
                                        dt_row, dn_norm_g)
    y_sample = _mlp(xs, o_nsa_s.reshape(1, ns, D_MODEL), o_dn_s[None], merges, mod_s, g2, gf, wo, wu, wd,
                    ns, "mlp_s")
    s_win = [jnp.concatenate([w[:, :, 1:], kvts[4 + t][0].T[:, :, None]], axis=2) for t, w in enumerate(wins)]

    return (y_prompt, y_sample.reshape(ns, 1, D_MODEL),
            kv_out(kvt[0]), kv_out(kvt[1]), kv_out(kvt[2]), kv_out(kvt[3]),
            kv_out(kvt[4][:, :, seq - wlen:]), kv_out(kvt[5][:, :, seq - wlen:]),
            p_conv[None], p_dn[None],
            kvts[0][0].T.reshape(ns, 1, KV_HEADS, HEAD_DIM)[None],
            kvts[1][0].T.reshape(ns, 1, KV_HEADS, HEAD_DIM)[None],
            kvts[2][0].T.reshape(ns, 1, KV_HEADS, HEAD_DIM)[None],
            kvts[3][0].T.reshape(ns, 1, KV_HEADS, HEAD_DIM)[None],
            kv_out(s_win[0]), kv_out(s_win[1]),
            conv_new.transpose(1, 0, 2)[None], s_dn[None])
```

```python
import functools

import numpy as np
import jax
import jax.numpy as jnp
from jax import lax
from jax.experimental import pallas as pl
from jax.experimental.pallas import tpu as pltpu

F32 = jnp.float32
BF16 = jnp.bfloat16

D_MODEL = 1024
N_HEADS = 16
HEAD_DIM = 64
KV_HEADS = 4
GROUP = N_HEADS // KV_HEADS
KV_WIDTH = KV_HEADS * HEAD_DIM
CMP_STRIDE = 16
CMP_LEN = 32
CMP_HIDDEN = 128
SLC_BLOCK = 64
SLC_TOPN = 16
WINDOW = 512
DN_HEADS = 8
DN_DK = 128
DN_QKV = 3072
CONV_W = 4
D_FF = 4096
EPS = 1e-6
NEG = -1e30
LANE = 128
VMEM_LIMIT = 56 * 1024 * 1024

A_LANE = 3 * N_HEADS
B_LANE = A_LANE + DN_HEADS


def _cparams(sem):
    return pltpu.CompilerParams(dimension_semantics=sem, vmem_limit_bytes=VMEM_LIMIT)


def _dot(a, b):
    return jnp.dot(a.astype(BF16), b.astype(BF16), preferred_element_type=F32)


def _dot_nt(a, b):
    return lax.dot_general(a.astype(BF16), b.astype(BF16), (((1,), (1,)), ((), ())),
                           preferred_element_type=F32)


def _split3(x):
    hi = x.astype(BF16)
    r = x - hi.astype(F32)
    mid = r.astype(BF16)
    lo = (r - mid.astype(F32)).astype(BF16)
    return hi, mid, lo


def _dot_exact_lhs(a01, x):
    a = a01.astype(BF16)
    hi, mid, lo = _split3(x)
    return (jnp.dot(a, hi, preferred_element_type=F32) + jnp.dot(a, mid, preferred_element_type=F32)
            + jnp.dot(a, lo, preferred_element_type=F32))


def _dot_exact_rhs(x, b01):
    b = b01.astype(BF16)
    hi, mid, lo = _split3(x)
    return (jnp.dot(hi, b, preferred_element_type=F32) + jnp.dot(mid, b, preferred_element_type=F32)
            + jnp.dot(lo, b, preferred_element_type=F32))


def _sigmoid(x):
    return 1.0 / (1.0 + jnp.exp(-x))


def _softplus(x):
    return jnp.maximum(x, 0.0) + jnp.log(1.0 + jnp.exp(-jnp.abs(x)))


def _norm_mod(x, g, sc, sh):
    y = x * lax.rsqrt(jnp.mean(x * x, axis=-1, keepdims=True) + EPS)
    return (y * g) * (1.0 + sc) + sh


def _ada_kernel(c_ref, w_ref, b_ref, o_ref):
    o_ref[...] = _dot(c_ref[...], w_ref[...]) + b_ref[...]


def _ada(c_all, w_ada, b_ada):
    rows = c_all.shape[0]
    n_out = w_ada.shape[1]
    tn = D_MODEL
    return pl.pallas_call(
        _ada_kernel,
        grid=(n_out // tn,),
        in_specs=[pl.BlockSpec((rows, D_MODEL), lambda j: (0, 0)),
                  pl.BlockSpec((D_MODEL, tn), lambda j: (0, j)),
                  pl.BlockSpec((1, tn), lambda j: (0, j))],
        out_specs=pl.BlockSpec((rows, tn), lambda j: (0, j)),
        out_shape=jax.ShapeDtypeStruct((rows, n_out), F32),
        compiler_params=_cparams(("arbitrary",)),
        name="ada",
    )(c_all, w_ada, b_ada)


PROJ_CHUNK = 512


def _proj_kernel(kinds, x_ref, mod_ref, g_ref, *refs):
    nw = len(kinds)
    w_refs, o_refs = refs[:nw], refs[nw:]
    h = _norm_mod(x_ref[...], g_ref[...], mod_ref[1], mod_ref[0]).astype(BF16)
    for kind, w_ref, o_ref in zip(kinds, w_refs, o_refs):
        if kind == "nn":
            width = w_ref.shape[1]
            for c in range(0, width, PROJ_CHUNK):
                e = min(c + PROJ_CHUNK, width)
                o_ref[:, c:e] = jnp.dot(h, w_ref[:, c:e], preferred_element_type=F32).astype(o_ref.dtype)
        else:
            width = w_ref.shape[0]
            for c in range(0, width, PROJ_CHUNK):
                e = min(c + PROJ_CHUNK, width)
                o_ref[c:e, :] = lax.dot_general(w_ref[c:e, :], h, (((1,), (1,)), ((), ())),
                                                preferred_element_type=F32).astype(o_ref.dtype)


def _proj(x, mod, g, weights, kinds, dtypes, tm, name):
    G, R, _ = x.shape
    rm = mod.shape[2]
    grid = (G, R // tm)
    in_specs = [pl.BlockSpec((None, tm, D_MODEL), lambda n, i: (n, i, 0)),
                pl.BlockSpec((None, 6, rm, D_MODEL), lambda n, i: (n, 0, 0, 0)),
                pl.BlockSpec((1, D_MODEL), lambda n, i: (0, 0))]
    out_specs, out_shape = [], []
    for w, kind, dt in zip(weights, kinds, dtypes):
        in_specs.append(pl.BlockSpec(w.shape, lambda n, i: (0, 0)))
        if kind == "nn":
            width = w.shape[1]
            out_specs.append(pl.BlockSpec((None, tm, width), lambda n, i: (n, i, 0)))
            out_shape.append(jax.ShapeDtypeStruct((G, R, width), dt))
        else:
            width = w.shape[0]
            out_specs.append(pl.BlockSpec((None, width, tm), lambda n, i: (n, 0, i)))
            out_shape.append(jax.ShapeDtypeStruct((G, width, R), dt))
    return pl.pallas_call(
        functools.partial(_proj_kernel, tuple(kinds)),
        grid=grid, in_specs=in_specs, out_specs=out_specs, out_shape=out_shape,
        compiler_params=_cparams(("parallel", "parallel")),
        name=name,
    )(x, mod, g, *weights)


def _compress_core(tile_fn, n_tiles, kp_ref, sh_ref, perm_ref, wab_ref, pe2_ref, w1f_ref, w2t_ref):
    n_chunks = n_tiles * (LANE // CMP_STRIDE)
    per_tile = LANE // CMP_STRIDE
    width = 2 * CMP_HIDDEN
    pe_term = _dot(pe2_ref[...], w1f_ref[...])[0:1]
    pe2 = jnp.concatenate([pe_term, pe_term], axis=1)
    n_pairs = KV_HEADS // 2
    for pair in range(n_pairs):
        for t in range(n_tiles):
            zs = _dot_nt(perm_ref[...], tile_fn(pair, t))
            r0 = pair * n_chunks + t * per_tile
            kp_ref[:, r0:r0 + per_tile, :] = zs.reshape(CMP_STRIDE, per_tile, LANE)
    acc = jnp.zeros((n_pairs * n_chunks, 2 * width), F32)
    for p in range(CMP_STRIDE):
        acc = acc + _dot(kp_ref[p], wab_ref[p])
    outs = []
    for pair in range(n_pairs):
        a = acc[pair * n_chunks:(pair + 1) * n_chunks]
        sh_ref[0:n_chunks, :] = a[:, width:]
        sh_ref[n_chunks:n_chunks + 8, :] = jnp.zeros((8, width), F32)
        hid = jax.nn.gelu(a[:, :width] + sh_ref[1:n_chunks + 1, :] + pe2)
        for gg in range(2):
            outs.append(_dot_nt(w2t_ref[...], hid[:, gg * CMP_HIDDEN:(gg + 1) * CMP_HIDDEN]))
    return outs


def _compress_kernel(n_tiles, kt_ref, perm_ref, wab_ref, pe2_ref, w1f_ref, w2t_ref, o_ref, kp_ref, sh_ref):
    tile = lambda pair, t: kt_ref[pair * LANE:(pair + 1) * LANE, t * LANE:(t + 1) * LANE]
    outs = _compress_core(tile, n_tiles, kp_ref, sh_ref, perm_ref, wab_ref, pe2_ref, w1f_ref, w2t_ref)
    for g in range(KV_HEADS):
        o_ref[g] = outs[g]


def _compress_scratch(n_chunks):
    return [pltpu.VMEM((CMP_STRIDE, (KV_HEADS // 2) * n_chunks, LANE), F32),
            pltpu.VMEM((n_chunks + 8, 2 * CMP_HIDDEN), F32)]


def _compress_prompt(kt, cw):
    n, _, length = kt.shape
    n_chunks = length // CMP_STRIDE
    const = lambda a: pl.BlockSpec(a.shape, lambda i: (0,) * a.ndim)
    return pl.pallas_call(
        functools.partial(_compress_kernel, length // LANE),
        grid=(n,),
        in_specs=[pl.BlockSpec((None, KV_WIDTH, length), lambda i: (i, 0, 0))] + [const(a) for a in cw],
        out_specs=pl.BlockSpec((None, KV_HEADS, HEAD_DIM, n_chunks), lambda i: (i, 0, 0, 0)),
        out_shape=jax.ShapeDtypeStruct((n, KV_HEADS, HEAD_DIM, n_chunks), F32),
        scratch_shapes=_compress_scratch(n_chunks),
        compiler_params=_cparams(("parallel",)),
        name="compress_prompt",
    )(kt, *cw)


def _compress_weights(pe, w1, w2):
    eye2 = jnp.eye(2, dtype=F32)
    wa = jnp.einsum("pde,ab->padbe", w1[:CMP_STRIDE], eye2).reshape(CMP_STRIDE, LANE, 2 * CMP_HIDDEN)
    wb = jnp.einsum("pde,ab->padbe", w1[CMP_STRIDE:], eye2).reshape(CMP_STRIDE, LANE, 2 * CMP_HIDDEN)
    wab = jnp.concatenate([wa, wb], axis=2)
    pe2 = jnp.zeros((8, CMP_LEN * HEAD_DIM), F32).at[0].set(pe.reshape(-1))
    w1f = w1.reshape(CMP_LEN * HEAD_DIM, CMP_HIDDEN)
    per_tile = LANE // CMP_STRIDE
    perm = np.zeros((LANE, LANE), np.float32)
    for p in range(CMP_STRIDE):
        for i in range(per_tile):
            perm[p * per_tile + i, CMP_STRIDE * i + p] = 1.0
    return (jnp.asarray(perm, dtype=BF16), wab.astype(BF16), pe2, w1f.astype(BF16), w2.T.astype(BF16))


def _overlap_matrix(n_cmp, n_blk, rows, cols):
    c0 = np.arange(n_cmp)[:, None] * CMP_STRIDE
    b0 = np.arange(n_blk)[None, :] * SLC_BLOCK
    inter = np.minimum(c0 + CMP_LEN, b0 + SLC_BLOCK) - np.maximum(c0, b0)
    ov = np.zeros((rows, cols), np.float32)
    ov[:n_cmp, :n_blk] = np.clip(inter, 0, None) / CMP_LEN
    return jnp.asarray(ov)


def _topk_select(imp, n_blk, top_n):
    lane = lax.broadcasted_iota(jnp.int32, imp.shape, 1)
    rank = jnp.zeros(imp.shape, F32)
    for i in range(n_blk):
        col = imp[:, i:i + 1]
        ahead = (col > imp) | ((col == imp) & (lane > i))
        rank = rank + ahead.astype(F32)
    return (rank < float(top_n)).astype(F32)


def _imp_matmul(psum, ov):
    hi = psum.astype(BF16)
    lo = (psum - hi.astype(F32)).astype(BF16)
    ovb = ov.astype(BF16)
    return jnp.dot(hi, ovb, preferred_element_type=F32) + jnp.dot(lo, ovb, preferred_element_type=F32)


NSA_TQ = 256
NSA_TK_SLC = 1024
NSA_RC = 256


def _nsa_prompt_kernel(seq, q_ref, gate_ref, kc_ref, vc_ref, ks_ref, vs_ref, kw_ref, vw_ref, ovt_ref, o_ref):
    tq = NSA_TQ
    g = pl.program_id(1)
    i = pl.program_id(2)
    s0 = i * tq
    n_cmp = seq // CMP_STRIDE - 1
    n_blk = seq // SLC_BLOCK
    top_n = min(SLC_TOPN, n_blk)

    q = q_ref[...].astype(F32) * (HEAD_DIM ** -0.5)
    qh = [q[:, h * HEAD_DIM:(h + 1) * HEAD_DIM].astype(BF16) for h in range(GROUP)]
    q4 = jnp.concatenate(qh, axis=0)
    qpos = s0 + lax.broadcasted_iota(jnp.int32, (tq, 1), 0)

    n_c = kc_ref.shape[1]
    s = _dot(q4, kc_ref[...]).reshape(GROUP, tq, n_c)
    cidx = lax.broadcasted_iota(jnp.int32, (tq, n_c), 1)
    valid = ((cidx * CMP_STRIDE + (CMP_LEN - 1)) <= qpos) & (cidx < n_cmp)
    sm = jnp.where(valid[None], s, NEG)
    m = jnp.max(sm, axis=-1, keepdims=True)
    p = jnp.where(valid[None], jnp.exp(sm - m), 0.0)
    p = p / jnp.maximum(jnp.sum(p, axis=-1, keepdims=True), 1e-30)
    o_cmp = _dot_nt(p.reshape(GROUP * tq, n_c), vc_ref[...])
    psum = p[0] + p[1] + p[2] + p[3]

    hi = psum.astype(BF16)
    lo = (psum - hi.astype(F32)).astype(BF16)
    imp_t = _dot_nt(ovt_ref[...], hi) + _dot_nt(ovt_ref[...], lo)
    nb8 = -(-n_blk // 8) * 8
    imp_t = imp_t[:nb8]
    blk = lax.broadcasted_iota(jnp.int32, (nb8, tq), 0)
    qrow = s0 + lax.broadcasted_iota(jnp.int32, (nb8, tq), 1)
    forced = (blk == qrow // SLC_BLOCK) | (blk == 0)
    causal = ((blk * SLC_BLOCK) <= qrow) & (blk < n_blk)
    imp_t = jnp.where(forced, jnp.inf, jnp.where(causal, imp_t, -jnp.inf))
    rank = jnp.zeros((nb8, tq), F32)
    for j in range(n_blk):
        row = imp_t[j:j + 1, :]
        ahead = (row > imp_t) | ((row == imp_t) & (blk > j))
        rank = rank + ahead.astype(F32)
    sel_t = (rank < float(top_n)).astype(F32)
    if nb8 < LANE:
        sel_t = jnp.concatenate([sel_t, jnp.zeros((LANE - nb8, tq), F32)], axis=0)
    sel = sel_t.T.astype(BF16)

    chunks = [(h, slice(r, r + NSA_RC)) for h in range(GROUP) for r in range(0, tq, NSA_RC)]

    def flash(k_ref, v_ref, tk, first_key, bias_fn):
        kt_hi = (s0 + tq - 1) // tk + 1
        kt_lo = first_key // tk

        def body(j, carry):
            kt = kt_hi - 1 - j
            k0 = pl.multiple_of(kt * tk, tk)
            kT = k_ref[:, pl.ds(k0, tk)].astype(BF16)
            vT = v_ref[:, pl.ds(k0, tk)].astype(BF16)
            kpos = k0 + lax.broadcasted_iota(jnp.int32, (tq, tk), 1)
            bias = bias_fn(kt, kpos, tk)
            out = []
            for c, (h, r) in enumerate(chunks):
                m, l, acc = carry[c]
                sm = _dot(qh[h][r], kT) + bias[r]
                m_new = jnp.maximum(m, jnp.max(sm, axis=-1, keepdims=True))
                alpha = jnp.exp(m - m_new)
                p = jnp.exp(sm - m_new)
                l = alpha * l + jnp.sum(p, axis=-1, keepdims=True)
                acc = alpha * acc + _dot_nt(p, vT)
                out.append((m_new, l, acc))
            return tuple(out)
        init = tuple((jnp.full((NSA_RC, 1), NEG, F32), jnp.zeros((NSA_RC, 1), F32),
                      jnp.zeros((NSA_RC, HEAD_DIM), F32)) for _ in chunks)
        res = lax.fori_loop(0, kt_hi - kt_lo, body, init)
        return jnp.concatenate([acc / jnp.maximum(l, 1e-30) for _, l, acc in res], axis=0)

    def slc_bias(kt, kpos, tk):
        row = lax.broadcasted_iota(jnp.int32, (LANE, tk), 0)
        col = lax.broadcasted_iota(jnp.int32, (LANE, tk), 1)
        expand = (row == kt * (tk // SLC_BLOCK) + col // SLC_BLOCK).astype(BF16)
        chosen = jnp.dot(sel, expand, preferred_element_type=F32) > 0.5
        return jnp.where(chosen & (kpos <= qpos), 0.0, NEG)

    o_slc = flash(ks_ref, vs_ref, NSA_TK_SLC, 0, slc_bias)

    span = min(WINDOW + tq, seq)
    w0 = pl.multiple_of(jnp.minimum(jnp.maximum(s0 - WINDOW, 0), seq - span), tq)
    kT = kw_ref[:, pl.ds(w0, span)].astype(BF16)
    vT = vw_ref[:, pl.ds(w0, span)].astype(BF16)
    kpos = w0 + lax.broadcasted_iota(jnp.int32, (tq, span), 1)
    bias = jnp.where((kpos <= qpos) & (kpos > qpos - WINDOW), 0.0, NEG)
    o_win = []
    for h, r in chunks:
        sm = _dot(qh[h][r], kT) + bias[r]
        p = jnp.exp(sm - jnp.max(sm, axis=-1, keepdims=True))
        o_win.append(_dot_nt(p, vT) / jnp.sum(p, axis=-1, keepdims=True))
    o_win = jnp.concatenate(o_win, axis=0)

    sg = _sigmoid(gate_ref[...])
    lane = lax.broadcasted_iota(jnp.int32, (tq, LANE), 1)
    for h in range(GROUP):
        base = g * (GROUP * 3) + h * 3
        gc = [jnp.sum(jnp.where(lane == base + b, sg, 0.0), axis=-1, keepdims=True) for b in range(3)]
        rows = slice(h * tq, (h + 1) * tq)
        out = gc[0] * o_cmp[rows] + gc[1] * o_slc[rows] + gc[2] * o_win[rows]
        o_ref[:, h * HEAD_DIM:(h + 1) * HEAD_DIM] = out


def _nsa_prompt(q, small, kct, vct, kst, vst, kwt, vwt):
    n, seq, _ = q.shape
    n_c = kct.shape[-1]
    assert NSA_TK_SLC % NSA_TQ == 0 and seq % NSA_TK_SLC == 0 and WINDOW % NSA_TQ == 0
    ov = _overlap_matrix(seq // CMP_STRIDE - 1, seq // SLC_BLOCK, n_c, LANE).T
    cw = GROUP * HEAD_DIM
    head_spec = lambda width: pl.BlockSpec((None, None, HEAD_DIM, width), lambda b, g, i: (b, g, 0, 0))
    r4 = lambda a: a.reshape(n, KV_HEADS, HEAD_DIM, seq)
    return pl.pallas_call(
        functools.partial(_nsa_prompt_kernel, seq),
        grid=(n, KV_HEADS, seq // NSA_TQ),
        in_specs=[pl.BlockSpec((None, NSA_TQ, cw), lambda b, g, i: (b, i, g)),
                  pl.BlockSpec((None, NSA_TQ, LANE), lambda b, g, i: (b, i, 0)),
                  head_spec(n_c), head_spec(n_c), head_spec(seq), head_spec(seq), head_spec(seq), head_spec(seq),
                  pl.BlockSpec(ov.shape, lambda b, g, i: (0, 0))],
        out_specs=pl.BlockSpec((None, NSA_TQ, cw), lambda b, g, i: (b, i, g)),
        out_shape=jax.ShapeDtypeStruct((n, seq, D_MODEL), F32),
        compiler_params=_cparams(("parallel", "parallel", "parallel")),
        name="nsa_prompt",
    )(q, small, kct, vct, r4(kst), r4(vst), r4(kwt), r4(vwt), ov)


DN_C = 128


def _dn_gates(small, alog_ref, dt_ref):
    g = -jnp.exp(alog_ref[...]) * _softplus(small + dt_ref[...])
    return g, _sigmoid(small)


def _l2n(x):
    return x * lax.rsqrt(jnp.sum(x * x, axis=-1, keepdims=True) + EPS)


def _dn_out(o, z, ng):
    on = o * lax.rsqrt(jnp.mean(o * o, axis=-1, keepdims=True) + EPS) * ng
    return on * (z * _sigmoid(z))


INV_BASE = 16


def _unit_lower_inverse(lmats, eye, ri, ci, c):
    same = (ri // INV_BASE) == (ci // INV_BASE)
    pws = [jnp.where(same, -l, 0.0) for l in lmats]
    xs = [eye + pw for pw in pws]
    for _ in range(int(np.log2(INV_BASE)) - 1):
        pws = [_dot(pw, pw) for pw in pws]
        xs = [x + _dot(x, pw) for x, pw in zip(xs, pws)]
    s = INV_BASE
    while s < c:
        off = ((ri // (2 * s)) == (ci // (2 * s))) & ((ri // s) != (ci // s))
        ts = [_dot(x, jnp.where(off, l, 0.0)) for x, l in zip(xs, lmats)]
        xs = [x - _dot(t, x) for x, t in zip(xs, ts)]
        s *= 2
    return xs


def _dn_prompt_kernel(qkv_ref, z_ref, small_ref, cw_ref, alog_ref, dt_ref, ng_ref, o_ref, st_ref, xbuf, s_ref):
    c = DN_C
    i = pl.program_id(1)
    last = pl.num_programs(1) - 1

    @pl.when(i == 0)
    def _():
        xbuf[0:8, :] = jnp.zeros((8, DN_QKV), F32)
        s_ref[...] = jnp.zeros_like(s_ref)

    xbuf[8:8 + c, :] = qkv_ref[...]
    y = (cw_ref[3:4, :] * xbuf[8:8 + c, :] + cw_ref[2:3, :] * xbuf[7:7 + c, :]
         + cw_ref[1:2, :] * xbuf[6:6 + c, :] + cw_ref[0:1, :] * xbuf[5:5 + c, :])
    xbuf[0:8, :] = xbuf[c:c + 8, :]
    y = y * _sigmoid(y)

    gl, bt = _dn_gates(small_ref[...], alog_ref, dt_ref)
    ri = lax.broadcasted_iota(jnp.int32, (c, c), 0)
    ci = lax.broadcasted_iota(jnp.int32, (c, c), 1)
    incl = ri >= ci
    strict = ri > ci
    gc_col = _dot_exact_lhs(incl.astype(F32), gl)
    gc_row = _dot_exact_rhs(gl.T, (ri <= ci).astype(F32))
    eye = (ri == ci).astype(F32)
    nd = DN_HEADS * DN_DK

    hs = range(DN_HEADS)
    sls = [slice(h * DN_DK, (h + 1) * DN_DK) for h in hs]
    q = [_l2n(y[:, sl]) * (DN_DK ** -0.5) for sl in sls]
    k = [_l2n(y[:, nd + h * DN_DK:nd + (h + 1) * DN_DK]) for h in hs]
    v = [y[:, 2 * nd + h * DN_DK:2 * nd + (h + 1) * DN_DK] for h in hs]
    gcc = [gc_col[:, A_LANE + h:A_LANE + h + 1] for h in hs]
    gcr = [gc_row[A_LANE + h:A_LANE + h + 1, :] for h in hs]
    beta = [bt[:, B_LANE + h:B_LANE + h + 1] for h in hs]
    decay = [jnp.exp(jnp.where(incl, gcc[h] - gcr[h], NEG)) for h in hs]
    kb = [k[h] * beta[h] for h in hs]
    lmat = [jnp.where(strict, _dot_nt(kb[h], k[h]) * decay[h], 0.0) for h in hs]
    a_intra = [_dot_nt(q[h], k[h]) * decay[h] for h in hs]
    tmat = _unit_lower_inverse(lmat, eye, ri, ci, c)
    eg = [jnp.exp(gcc[h]) for h in hs]
    u = [_dot(tmat[h], v[h] * beta[h]) for h in hs]
    w = [_dot(tmat[h], kb[h] * eg[h]) for h in hs]
    g_last = [gcr[h][:, c - 1:c] for h in hs]
    k_dec_t = [(k[h] * jnp.exp(g_last[h] - gcc[h])).T for h in hs]
    state = [s_ref[h] for h in hs]
    v_new = [u[h] - _dot(w[h], state[h]) for h in hs]
    o = [_dot(q[h] * eg[h], state[h]) + _dot(a_intra[h], v_new[h]) for h in hs]
    for h in hs:
        s_ref[h] = state[h] * jnp.exp(g_last[h]) + _dot(k_dec_t[h], v_new[h])
    for h in hs:
        o_ref[:, sls[h]] = _dn_out(o[h], z_ref[:, sls[h]], ng_ref[...])

    @pl.when(i == last)
    def _():
        st_ref[...] = s_ref[...]


def _dn_prompt(qkv, z, small, conv_w, alog_row, dt_row, norm_g):
    n, seq, _ = qkv.shape
    c = DN_C
    const = lambda a: pl.BlockSpec(a.shape, lambda b, i: (0,) * a.ndim)
    return pl.pallas_call(
        _dn_prompt_kernel,
        grid=(n, seq // c),
        in_specs=[pl.BlockSpec((None, c, DN_QKV), lambda b, i: (b, i, 0)),
                  pl.BlockSpec((None, c, D_MODEL), lambda b, i: (b, i, 0)),
                  pl.BlockSpec((None, c, LANE), lambda b, i: (b, i, 0)),
                  const(conv_w), const(alog_row), const(dt_row), const(norm_g)],
        out_specs=[pl.BlockSpec((None, c, D_MODEL), lambda b, i: (b, i, 0)),
                   pl.BlockSpec((None, DN_HEADS, DN_DK, DN_DK), lambda b, i: (b, 0, 0, 0))],
        out_shape=[jax.ShapeDtypeStruct((n, seq, D_MODEL), F32),
                   jax.ShapeDtypeStruct((n, DN_HEADS, DN_DK, DN_DK), F32)],
        scratch_shapes=[pltpu.VMEM((c + 8, DN_QKV), F32), pltpu.VMEM((DN_HEADS, DN_DK, DN_DK), F32)],
        compiler_params=_cparams(("parallel", "arbitrary")),
        name="dn_prompt",
    )(qkv, z, small, conv_w, alog_row, dt_row, norm_g)


MLP_TF = 1024


def _mlp_kernel(x_ref, on_ref, od_ref, mg_ref, mod_ref, g2_ref, gf_ref, wo_ref, wu_ref, wd_ref, y_ref,
                x1_ref, h2_ref, acc_ref):
    j = pl.program_id(2)
    last = pl.num_programs(2) - 1

    @pl.when(j == 0)
    def _():
        ga = _sigmoid(mg_ref[:, :D_MODEL])
        gb = _sigmoid(mg_ref[:, D_MODEL:])
        mixed = ga * on_ref[...] + gb * od_ref[...]
        x1 = x_ref[...] + mod_ref[2] * _dot(mixed, wo_ref[...])
        x1_ref[...] = x1
        h2_ref[...] = _norm_mod(x1, g2_ref[...], mod_ref[4], mod_ref[3]).astype(BF16)
        acc_ref[...] = jnp.zeros_like(acc_ref)

    up = jnp.dot(h2_ref[...], wu_ref[...], preferred_element_type=F32)
    act = jnp.square(jnp.maximum(up, 0.0))
    acc_ref[...] += _dot(act, wd_ref[...])

    @pl.when(j == last)
    def _():
        x2 = x1_ref[...] + mod_ref[5] * acc_ref[...]
        y_ref[...] = x2 * lax.rsqrt(jnp.mean(x2 * x2, axis=-1, keepdims=True) + EPS) * gf_ref[...]


def _mlp(x, o_nsa, o_dn, merge, mod, g2, gf, w_out, w_up, w_down, tm, name):
    G, R, _ = x.shape
    rm = mod.shape[2]
    tf = MLP_TF
    row = lambda width: pl.BlockSpec((None, tm, width), lambda n, i, j: (n, i, 0))
    return pl.pallas_call(
        _mlp_kernel,
        grid=(G, R // tm, D_FF // tf),
        in_specs=[row(D_MODEL), row(D_MODEL), row(D_MODEL), row(2 * D_MODEL),
                  pl.BlockSpec((None, 6, rm, D_MODEL), lambda n, i, j: (n, 0, 0, 0)),
                  pl.BlockSpec((1, D_MODEL), lambda n, i, j: (0, 0)),
                  pl.BlockSpec((1, D_MODEL), lambda n, i, j: (0, 0)),
                  pl.BlockSpec((D_MODEL, D_MODEL), lambda n, i, j: (0, 0)),
                  pl.BlockSpec((D_MODEL, tf), lambda n, i, j: (0, j)),
                  pl.BlockSpec((tf, D_MODEL), lambda n, i, j: (j, 0))],
        out_specs=row(D_MODEL),
        out_shape=jax.ShapeDtypeStruct((G, R, D_MODEL), F32),
        scratch_shapes=[pltpu.VMEM((tm, D_MODEL), F32), pltpu.VMEM((tm, D_MODEL), BF16),
                        pltpu.VMEM((tm, D_MODEL), F32)],
        compiler_params=_cparams(("parallel", "parallel", "arbitrary")),
        name=name,
    )(x, o_nsa, o_dn, merge, mod, g2, gf, w_out, w_up, w_down)


def _nsa_sample_kernel(n_pages, page, win, *refs):
    n_in = 4 * n_pages
    pt_ref = refs[0]
    del pt_ref
    pools = [refs[1 + t * n_pages:1 + (t + 1) * n_pages] for t in range(4)]
    (q_ref, gate_ref, new_ref, kwc_ref, vwc_ref, ov_ref, ex_ref) = refs[1 + n_in:1 + n_in + 7]
    cwk = refs[1 + n_in + 7:1 + n_in + 12]
    cwv = refs[1 + n_in + 12:1 + n_in + 17]
    o_ref, kwo_ref, vwo_ref = refs[1 + n_in + 17:1 + n_in + 20]
    full_a, full_b, kp_ref, sh_ref = refs[1 + n_in + 20:]

    past = n_pages * page
    total = past + 1
    lp = -(-total // SLC_BLOCK) * SLC_BLOCK
    n_cmp = lp // CMP_STRIDE - 1
    n_blk = lp // SLC_BLOCK
    top_n = min(SLC_TOPN, n_blk)
    length = full_a.shape[1]
    n_c = length // CMP_STRIDE
    n_tiles = length // LANE

    def fill(full_ref, page_refs, t):
        for j in range(n_pages):
            full_ref[:, j * page:(j + 1) * page] = page_refs[j][...]
        full_ref[:, past:length] = jnp.zeros((KV_WIDTH, length - past), F32)
        full_ref[:, past:past + 1] = new_ref[t]

    def cmp_tiles(page_refs, t):
        def tile(pair, j):
            rows = slice(pair * LANE, (pair + 1) * LANE)
            if j < n_pages:
                return page_refs[j][rows, :]
            if j == n_pages:
                lane = lax.broadcasted_iota(jnp.int32, (LANE, LANE), 1)
                return jnp.where(lane == 0, new_ref[t][rows, :], 0.0)
            return jnp.zeros((LANE, LANE), F32)
        return tile

    q16 = (q_ref[...].astype(F32) * (HEAD_DIM ** -0.5)).astype(BF16)
    head_group = lax.broadcasted_iota(jnp.int32, (N_HEADS, 1), 0) // GROUP

    def by_group(vals):
        out = vals[KV_HEADS - 1]
        for g in range(KV_HEADS - 2, -1, -1):
            out = jnp.where(head_group == g, vals[g], out)
        return out

    def attend(kts, vts, valid):
        s = by_group([_dot(q16, kt) for kt in kts])
        sm = jnp.where(valid, s, NEG)
        m = jnp.max(sm, axis=-1, keepdims=True)
        p = jnp.where(valid, jnp.exp(sm - m), 0.0)
        p = p / jnp.maximum(jnp.sum(p, axis=-1, keepdims=True), 1e-30)
        o = by_group([_dot_nt(p, vt) for vt in vts])
        return p, o

    kcs = _compress_core(cmp_tiles(pools[0], 0), n_tiles, kp_ref, sh_ref, *cwk)
    vcs = _compress_core(cmp_tiles(pools[1], 1), n_tiles, kp_ref, sh_ref, *cwv)
    cidx = lax.broadcasted_iota(jnp.int32, (1, n_c), 1)
    valid_c = ((cidx * CMP_STRIDE + (CMP_LEN - 1)) <= past) & (cidx < n_cmp)
    p_cmp, o_cmp = attend(kcs, vcs, valid_c)
    imp_h = _imp_matmul(p_cmp, ov_ref[...])
    hr = lax.broadcasted_iota(jnp.int32, (N_HEADS, N_HEADS), 0) // GROUP
    hc = lax.broadcasted_iota(jnp.int32, (N_HEADS, N_HEADS), 1) // GROUP
    imp = _dot_exact_lhs((hr == hc).astype(F32), imp_h)
    blk = lax.broadcasted_iota(jnp.int32, (N_HEADS, LANE), 1)
    forced = (blk == past // SLC_BLOCK) | (blk == 0)
    causal = (blk * SLC_BLOCK) <= past
    imp = jnp.where(forced, jnp.inf, jnp.where(causal & (blk < n_blk), imp, -jnp.inf))
    sel = _topk_select(imp, n_blk, top_n).astype(BF16)

    fill(full_a, pools[2], 2)
    fill(full_b, pools[3], 3)
    kpos = lax.broadcasted_iota(jnp.int32, (1, length), 1)
    chosen = jnp.dot(sel, ex_ref[...], preferred_element_type=F32) > 0.5
    valid_s = chosen & (kpos <= past)
    kts = [full_a[g * HEAD_DIM:(g + 1) * HEAD_DIM, :] for g in range(KV_HEADS)]
    vts = [full_b[g * HEAD_DIM:(g + 1) * HEAD_DIM, :] for g in range(KV_HEADS)]
    _, o_slc = attend(kts, vts, valid_s)

    lane_w = lax.broadcasted_iota(jnp.int32, (KV_WIDTH, win), 1)
    kw = jnp.where(lane_w == win - 1, new_ref[4], pltpu.roll(kwc_ref[...], win - 1, axis=1))
    vw = jnp.where(lane_w == win - 1, new_ref[5], pltpu.roll(vwc_ref[...], win - 1, axis=1))
    kwo_ref[...] = kw
    vwo_ref[...] = vw
    wpos = past - win + 1 + lax.broadcasted_iota(jnp.int32, (1, win), 1)
    valid_w = (wpos <= past) & (wpos > past - WINDOW) & (wpos >= 0)
    kts = [kw[g * HEAD_DIM:(g + 1) * HEAD_DIM, :] for g in range(KV_HEADS)]
    vts = [vw[g * HEAD_DIM:(g + 1) * HEAD_DIM, :] for g in range(KV_HEADS)]
    _, o_win = attend(kts, vts, valid_w)

    sg = _sigmoid(gate_ref[...])
    o_ref[...] = sg[:, 0:1] * o_cmp + sg[:, 1:2] * o_slc + sg[:, 2:3] * o_win


def _nsa_sample(page_table, pools, q16, gates, newcols, kwc, vwc, cwk, cwv):
    n_seq, n_pages = page_table.shape
    page = pools[0].shape[-1]
    win = kwc.shape[-1]
    past = n_pages * page
    lp = -(-(past + 1) // SLC_BLOCK) * SLC_BLOCK
    length = -(-lp // LANE) * LANE
    n_c = length // CMP_STRIDE
    ov = _overlap_matrix(lp // CMP_STRIDE - 1, lp // SLC_BLOCK, n_c, LANE)
    ex = np.zeros((LANE, length), np.float32)
    ex[np.arange(length) // SLC_BLOCK, np.arange(length)] = 1.0
    ex = jnp.asarray(ex, dtype=BF16)
    assert win == WINDOW and page == LANE

    in_specs = []
    for _ in range(4):
        for j in range(n_pages):
            in_specs.append(pl.BlockSpec((None, KV_WIDTH, page), functools.partial(
                lambda s, pt, jj: (pt[s, jj], 0, 0), jj=j)))
    const = lambda a: pl.BlockSpec(a.shape, lambda s, pt: (0,) * a.ndim)
    in_specs += [pl.BlockSpec((None, N_HEADS, HEAD_DIM), lambda s, pt: (s, 0, 0)),
                 pl.BlockSpec((None, N_HEADS, 3), lambda s, pt: (s, 0, 0)),
                 pl.BlockSpec((None, 6, KV_WIDTH, 1), lambda s, pt: (s, 0, 0, 0)),
                 pl.BlockSpec((None, KV_WIDTH, win), lambda s, pt: (s, 0, 0)),
                 pl.BlockSpec((None, KV_WIDTH, win), lambda s, pt: (s, 0, 0)),
                 const(ov), const(ex)] + [const(a) for a in cwk] + [const(a) for a in cwv]
    operands = []
    for t in range(4):
        operands += [pools[t]] * n_pages
    operands += [q16, gates, newcols, kwc, vwc, ov, ex, *cwk, *cwv]
    win_spec = pl.BlockSpec((None, KV_WIDTH, win), lambda s, pt: (s, 0, 0))
    grid_spec = pltpu.PrefetchScalarGridSpec(
        num_scalar_prefetch=1, grid=(n_seq,), in_specs=in_specs,
        out_specs=[pl.BlockSpec((None, N_HEADS, HEAD_DIM), lambda s, pt: (s, 0, 0)), win_spec, win_spec],
        scratch_shapes=[pltpu.VMEM((KV_WIDTH, length), F32), pltpu.VMEM((KV_WIDTH, length), F32)]
        + _compress_scratch(n_c))
    return pl.pallas_call(
        functools.partial(_nsa_sample_kernel, n_pages, page, win),
        grid_spec=grid_spec,
        out_shape=[jax.ShapeDtypeStruct((n_seq, N_HEADS, HEAD_DIM), F32),
                   jax.ShapeDtypeStruct((n_seq, KV_WIDTH, win), F32),
                   jax.ShapeDtypeStruct((n_seq, KV_WIDTH, win), F32)],
        compiler_params=_cparams(("arbitrary",)),
        name="nsa_sample",
    )(page_table, *operands)


DN_SB = 8


def _dn_sample_kernel(qkv_ref, conv_ref, z_ref, small_ref, st_ref, cw_ref, alog_ref, dt_ref, ng_ref,
                      o_ref, conv_o_ref, st_o_ref):
    x = qkv_ref[...]
    y = (cw_ref[0:1, :] * conv_ref[0] + cw_ref[1:2, :] * conv_ref[1] + cw_ref[2:3, :] * conv_ref[2]
         + cw_ref[3:4, :] * x)
    conv_o_ref[0] = conv_ref[1]
    conv_o_ref[1] = conv_ref[2]
    conv_o_ref[2] = x
    y = y * _sigmoid(y)
    gl, bt = _dn_gates(small_ref[...], alog_ref, dt_ref)
    nd = DN_HEADS * DN_DK
    ri = lax.broadcasted_iota(jnp.int32, (DN_DK, DN_DK), 0)
    ci = lax.broadcasted_iota(jnp.int32, (DN_DK, DN_DK), 1)
    eye = ri == ci
    for h in range(DN_HEADS):
        sl = slice(h * DN_DK, (h + 1) * DN_DK)
        q = _l2n(y[:, sl]) * (DN_DK ** -0.5)
        k = _l2n(y[:, nd + h * DN_DK:nd + (h + 1) * DN_DK])
        v = y[:, 2 * nd + h * DN_DK:2 * nd + (h + 1) * DN_DK]
        g = gl[:, A_LANE + h:A_LANE + h + 1]
        beta = bt[:, B_LANE + h:B_LANE + h + 1]
        eg = jnp.exp(g)
        u = v * beta
        w = k * beta * eg
        a_intra = jnp.sum(q * k, axis=-1, keepdims=True)
        q_dec = q * eg
        z = z_ref[:, sl]
        for b in range(DN_SB):
            state = st_ref[b, h]
            lhs = jnp.concatenate([w[b:b + 1], q_dec[b:b + 1], jnp.zeros((6, DN_DK), F32)], axis=0)
            res = _dot(lhs, state)
            v_new = u[b:b + 1] - res[0:1]
            o = res[1:2] + a_intra[b:b + 1] * v_new
            k_col = jnp.sum(jnp.where(eye, k[b:b + 1], 0.0), axis=-1, keepdims=True)
            st_o_ref[b, h] = state * eg[b:b + 1] + k_col * v_new
            o_ref[b:b + 1, sl] = _dn_out(o, z[b:b + 1], ng_ref[...])


def _dn_sample(qkv, conv_t, z, small, state, conv_w, alog_row, dt_row, norm_g):
    n_seq = qkv.shape[0]
    sb = DN_SB
    const = lambda a: pl.BlockSpec(a.shape, lambda i: (0,) * a.ndim)
    return pl.pallas_call(
        _dn_sample_kernel,
        grid=(n_seq // sb,),
        in_specs=[pl.BlockSpec((sb, DN_QKV), lambda i: (i, 0)),
                  pl.BlockSpec((CONV_W - 1, sb, DN_QKV), lambda i: (0, i, 0)),
                  pl.BlockSpec((sb, D_MODEL), lambda i: (i, 0)),
                  pl.BlockSpec((sb, LANE), lambda i: (i, 0)),
                  pl.BlockSpec((sb, DN_HEADS, DN_DK, DN_DK), lambda i: (i, 0, 0, 0)),
                  const(conv_w), const(alog_row), const(dt_row), const(norm_g)],
        out_specs=[pl.BlockSpec((sb, D_MODEL), lambda i: (i, 0)),
                   pl.BlockSpec((CONV_W - 1, sb, DN_QKV), lambda i: (0, i, 0)),
                   pl.BlockSpec((sb, DN_HEADS, DN_DK, DN_DK), lambda i: (i, 0, 0, 0))],
        out_shape=[jax.ShapeDtypeStruct((n_seq, D_MODEL), F32),
                   jax.ShapeDtypeStruct((CONV_W - 1, n_seq, DN_QKV), F32),
                   jax.ShapeDtypeStruct(state.shape, F32)],
        compiler_params=_cparams(("parallel",)),
        name="dn_sample",
    )(qkv, conv_t, z, small, state, conv_w, alog_row, dt_row, norm_g)


def _pick_tile(rows, pref):
    t = min(rows, pref)
    while rows % t:
        t //= 2
    return t


def _layer_weights(w_in):
    w = w_in.astype(BF16)
    o = 0
    wq = w[:, o:o + D_MODEL]; o += D_MODEL
    wkv = [w[:, o + t * KV_WIDTH:o + (t + 1) * KV_WIDTH].T for t in range(6)]; o += 6 * KV_WIDTH
    wg = w[:, o:o + 3 * N_HEADS]; o += 3 * N_HEADS
    wqkv = w[:, o:o + DN_QKV]; o += DN_QKV
    wz = w[:, o:o + D_MODEL]; o += D_MODEL
    wa = w[:, o:o + DN_HEADS]; o += DN_HEADS
    wb = w[:, o:o + DN_HEADS]; o += DN_HEADS
    wm = w[:, o:o + 2 * D_MODEL]
    wsmall = jnp.concatenate([wg, wa, wb, jnp.zeros((D_MODEL, LANE - B_LANE - DN_HEADS), BF16)], axis=1)
    return wq, wkv, wqkv, wz, wm, wsmall


def kernel(x_prompt, x_sample, c_prompt, c_sample, cache_k_cmp, cache_v_cmp, cache_k_slc, cache_v_slc, cache_k_win, cache_v_win, state_conv, state_dn, page_table, w_ada, b_ada, norm1_g, norm2_g, w_in, cmp_pe_k, cmp_w1_k, cmp_w2_k, cmp_pe_v, cmp_w1_v, cmp_w2_v, dn_conv_w, dn_a_log, dn_dt_bias, dn_norm_g, w_out, w_up, w_down, final_g):
    assert w_ada.shape[0] == 1, "single-layer stack"
    nb, seq, _ = x_prompt.shape
    ns = x_sample.shape[0]
    assert x_sample.shape[1] == 1

    n_c = nb + ns
    rows = -(-n_c // 8) * 8
    c_all = jnp.concatenate([c_prompt, c_sample, jnp.zeros((rows - n_c, D_MODEL), F32)], axis=0)
    mod = _ada(c_all, w_ada[0], b_ada)
    mod_p = mod[:nb].reshape(nb, 6, 1, D_MODEL)
    mod_s = mod[nb:n_c].reshape(ns, 6, D_MODEL).transpose(1, 0, 2)[None]

    wq, wkv, wqkv, wz, wm, wsmall = _layer_weights(w_in[0])
    g1 = norm1_g
    g2 = norm2_g
    gf = final_g.reshape(1, D_MODEL)
    alog_row = jnp.zeros((1, LANE), F32).at[0, A_LANE:A_LANE + DN_HEADS].set(dn_a_log[0])
    dt_row = jnp.zeros((1, LANE), F32).at[0, A_LANE:A_LANE + DN_HEADS].set(dn_dt_bias[0])
    conv_w = dn_conv_w[0]
    cwk = _compress_weights(cmp_pe_k[0], cmp_w1_k[0], cmp_w2_k[0])
    cwv = _compress_weights(cmp_pe_v[0], cmp_w1_v[0], cmp_w2_v[0])
    wo = w_out[0].astype(BF16)
    wu = w_up[0].astype(BF16)
    wd = w_down[0].astype(BF16)

    def project(x, modg, tm, tag):
        q, *kvt = _proj(x, modg, g1, [wq] + wkv, ["nn"] + ["nt"] * 6, [BF16] + [F32] * 6, tm, "proj_a_" + tag)
        (qkv,) = _proj(x, modg, g1, [wqkv], ["nn"], [F32], tm, "proj_b_" + tag)
        z, merge, small = _proj(x, modg, g1, [wz, wm, wsmall], ["nn"] * 3, [F32] * 3, tm, "proj_c_" + tag)
        return q, kvt, qkv, z, merge, small

    def kv_out(a):
        n, _, length = a.shape
        return a.reshape(n, KV_HEADS, HEAD_DIM, length).transpose(0, 3, 1, 2)[None]

    tm_p = _pick_tile(seq, 512)
    q, kvt, qkv, z, merge, small = project(x_prompt, mod_p, tm_p, "p")
    kct = _compress_prompt(kvt[0], cwk)
    vct = _compress_prompt(kvt[1], cwv)
    o_nsa = _nsa_prompt(q, small, kct, vct, kvt[2], kvt[3], kvt[4], kvt[5])
    o_dn, p_dn = _dn_prompt(qkv, z, small, conv_w, alog_row, dt_row, dn_norm_g)
    y_prompt = _mlp(x_prompt, o_nsa, o_dn, merge, mod_p, g2, gf, wo, wu, wd, tm_p, "mlp_p")
    wlen = min(WINDOW, seq)
    p_conv = qkv[:, seq - (CONV_W - 1):, :]

    xs = x_sample.reshape(1, ns, D_MODEL)
    qs, kvts, qkvs, zs, merges, smalls = project(xs, mod_s, ns, "s")
    pools = [c[0].transpose(0, 2, 3, 1).reshape(c.shape[1], KV_WIDTH, c.shape[2])
             for c in (cache_k_cmp, cache_v_cmp, cache_k_slc, cache_v_slc)]
    wins = [c[0].transpose(0, 2, 3, 1).reshape(ns, KV_WIDTH, c.shape[2]) for c in (cache_k_win, cache_v_win)]
    newcols = jnp.stack([a[0].T for a in kvts], axis=1)[..., None]
    q16 = qs.reshape(ns, N_HEADS, HEAD_DIM)
    gates = smalls[0, :, :3 * N_HEADS].reshape(ns, N_HEADS, 3)
    o_nsa_s, kw_new, vw_new = _nsa_sample(page_table, pools, q16, gates, newcols, wins[0], wins[1], cwk, cwv)
    conv_t = state_conv[0].transpose(1, 0, 2)
    o_dn_s, conv_new, s_dn = _dn_sample(qkvs[0], conv_t, zs[0], smalls[0], state_dn[0], conv_w, alog_row,
                                        dt_row, dn_norm_g)
    y_sample = _mlp(xs, o_nsa_s.reshape(1, ns, D_MODEL), o_dn_s[None], merges, mod_s, g2, gf, wo, wu, wd,
                    ns, "mlp_s")
    s_win = [kw_new, vw_new]

    return (y_prompt, y_sample.reshape(ns, 1, D_MODEL),
            kv_out(kvt[0]), kv_out(kvt[1]), kv_out(kvt[2]), kv_out(kvt[3]),
            kv_out(kvt[4][:, :, seq - wlen:]), kv_out(kvt[5][:, :, seq - wlen:]),
            p_conv[None], p_dn[None],
            kvts[0][0].T.reshape(ns, 1, KV_HEADS, HEAD_DIM)[None],
            kvts[1][0].T.reshape(ns, 1, KV_HEADS, HEAD_DIM)[None],
            kvts[2][0].T.reshape(ns, 1, KV_HEADS, HEAD_DIM)[None],
            kvts[3][0].T.reshape(ns, 1, KV_HEADS, HEAD_DIM)[None],
            kv_out(s_win[0]), kv_out(s_win[1]),
            conv_new.transpose(1, 0, 2)[None], s_dn[None])
```

```python
import functools

import numpy as np
import jax
import jax.numpy as jnp
from jax import lax
from jax.experimental import pallas as pl
from jax.experimental.pallas import tpu as pltpu

F32 = jnp.float32
BF16 = jnp.bfloat16

D_MODEL = 1024
N_HEADS = 16
HEAD_DIM = 64
KV_HEADS = 4
GROUP = N_HEADS // KV_HEADS
KV_WIDTH = KV_HEADS * HEAD_DIM
CMP_STRIDE = 16
CMP_LEN = 32
CMP_HIDDEN = 128
SLC_BLOCK = 64
SLC_TOPN = 16
WINDOW = 512
DN_HEADS = 8
DN_DK = 128
DN_QKV = 3072
CONV_W = 4
D_FF = 4096
EPS = 1e-6
NEG = -1e30
LANE = 128
VMEM_LIMIT = 56 * 1024 * 1024

A_LANE = 3 * N_HEADS
B_LANE = A_LANE + DN_HEADS


def _cparams(sem):
    return pltpu.CompilerParams(dimension_semantics=sem, vmem_limit_bytes=VMEM_LIMIT)


def _dot(a, b):
    return jnp.dot(a.astype(BF16), b.astype(BF16), preferred_element_type=F32)


def _dot_nt(a, b):
    return lax.dot_general(a.astype(BF16), b.astype(BF16), (((1,), (1,)), ((), ())),
                           preferred_element_type=F32)


def _split3(x):
    hi = x.astype(BF16)
    r = x - hi.astype(F32)
    mid = r.astype(BF16)
    lo = (r - mid.astype(F32)).astype(BF16)
    return hi, mid, lo


def _dot_exact_lhs(a01, x):
    a = a01.astype(BF16)
    hi, mid, lo = _split3(x)
    return (jnp.dot(a, hi, preferred_element_type=F32) + jnp.dot(a, mid, preferred_element_type=F32)
            + jnp.dot(a, lo, preferred_element_type=F32))


def _dot_exact_rhs(x, b01):
    b = b01.astype(BF16)
    hi, mid, lo = _split3(x)
    return (jnp.dot(hi, b, preferred_element_type=F32) + jnp.dot(mid, b, preferred_element_type=F32)
            + jnp.dot(lo, b, preferred_element_type=F32))


def _sigmoid(x):
    return 1.0 / (1.0 + jnp.exp(-x))


def _softplus(x):
    return jnp.maximum(x, 0.0) + jnp.log(1.0 + jnp.exp(-jnp.abs(x)))


def _norm_mod(x, g, sc, sh):
    y = x * lax.rsqrt(jnp.mean(x * x, axis=-1, keepdims=True) + EPS)
    return (y * g) * (1.0 + sc) + sh


def _ada_kernel(c_ref, w_ref, b_ref, o_ref):
    o_ref[...] = _dot(c_ref[...], w_ref[...]) + b_ref[...]


def _ada(c_all, w_ada, b_ada):
    rows = c_all.shape[0]
    n_out = w_ada.shape[1]
    tn = D_MODEL
    return pl.pallas_call(
        _ada_kernel,
        grid=(n_out // tn,),
        in_specs=[pl.BlockSpec((rows, D_MODEL), lambda j: (0, 0)),
                  pl.BlockSpec((D_MODEL, tn), lambda j: (0, j)),
                  pl.BlockSpec((1, tn), lambda j: (0, j))],
        out_specs=pl.BlockSpec((rows, tn), lambda j: (0, j)),
        out_shape=jax.ShapeDtypeStruct((rows, n_out), F32),
        compiler_params=_cparams(("arbitrary",)),
        name="ada",
    )(c_all, w_ada, b_ada)


PROJ_CHUNK = 512


def _proj_kernel(kinds, x_ref, mod_ref, g_ref, *refs):
    nw = len(kinds)
    w_refs, o_refs = refs[:nw], refs[nw:]
    h = _norm_mod(x_ref[...], g_ref[...], mod_ref[1], mod_ref[0]).astype(BF16)
    for kind, w_ref, o_ref in zip(kinds, w_refs, o_refs):
        if kind == "nn":
            width = w_ref.shape[1]
            for c in range(0, width, PROJ_CHUNK):
                e = min(c + PROJ_CHUNK, width)
                o_ref[:, c:e] = jnp.dot(h, w_ref[:, c:e], preferred_element_type=F32).astype(o_ref.dtype)
        else:
            width = w_ref.shape[0]
            for c in range(0, width, PROJ_CHUNK):
                e = min(c + PROJ_CHUNK, width)
                o_ref[c:e, :] = lax.dot_general(w_ref[c:e, :], h, (((1,), (1,)), ((), ())),
                                                preferred_element_type=F32).astype(o_ref.dtype)


def _proj(x, mod, g, weights, kinds, dtypes, tm, name):
    G, R, _ = x.shape
    rm = mod.shape[2]
    grid = (G, R // tm)
    in_specs = [pl.BlockSpec((None, tm, D_MODEL), lambda n, i: (n, i, 0)),
                pl.BlockSpec((None, 6, rm, D_MODEL), lambda n, i: (n, 0, 0, 0)),
                pl.BlockSpec((1, D_MODEL), lambda n, i: (0, 0))]
    out_specs, out_shape = [], []
    for w, kind, dt in zip(weights, kinds, dtypes):
        in_specs.append(pl.BlockSpec(w.shape, lambda n, i: (0, 0)))
        if kind == "nn":
            width = w.shape[1]
            out_specs.append(pl.BlockSpec((None, tm, width), lambda n, i: (n, i, 0)))
            out_shape.append(jax.ShapeDtypeStruct((G, R, width), dt))
        else:
            width = w.shape[0]
            out_specs.append(pl.BlockSpec((None, width, tm), lambda n, i: (n, 0, i)))
            out_shape.append(jax.ShapeDtypeStruct((G, width, R), dt))
    return pl.pallas_call(
        functools.partial(_proj_kernel, tuple(kinds)),
        grid=grid, in_specs=in_specs, out_specs=out_specs, out_shape=out_shape,
        compiler_params=_cparams(("parallel", "parallel")),
        name=name,
    )(x, mod, g, *weights)


def _compress_core(jobs, n_tiles):
    n_chunks = n_tiles * (LANE // CMP_STRIDE)
    per_tile = LANE // CMP_STRIDE
    width = 2 * CMP_HIDDEN
    n_pairs = KV_HEADS // 2
    pe2s = []
    for _, _, _, (_, _, pe2_ref, w1f_ref, _) in jobs:
        pe_term = _dot(pe2_ref[...], w1f_ref[...])[0:1]
        pe2s.append(jnp.concatenate([pe_term, pe_term], axis=1))
    for pair in range(n_pairs):
        for t in range(n_tiles):
            for tile_fn, kp_ref, _, (perm_ref, _, _, _, _) in jobs:
                zs = _dot_nt(perm_ref[...], tile_fn(pair, t))
                r0 = pair * n_chunks + t * per_tile
                for p in range(CMP_STRIDE):
                    kp_ref[r0:r0 + per_tile, p * LANE:(p + 1) * LANE] = zs[p * per_tile:(p + 1) * per_tile]
    accs = [_dot(kp_ref[...], w[1][...]) for _, kp_ref, _, w in jobs]
    hids = []
    for pair in range(n_pairs):
        for (_, _, sh_ref, _), acc, pe2 in zip(jobs, accs, pe2s):
            a = acc[pair * n_chunks:(pair + 1) * n_chunks]
            sh_ref[pair, 0:n_chunks, :] = a[:, width:]
            sh_ref[pair, n_chunks:n_chunks + 8, :] = jnp.zeros((8, width), F32)
            hids.append(jax.nn.gelu(a[:, :width] + sh_ref[pair, 1:n_chunks + 1, :] + pe2))
    outs = [[] for _ in jobs]
    for pair in range(n_pairs):
        for gg in range(2):
            for j, (_, _, _, w) in enumerate(jobs):
                hid = hids[pair * len(jobs) + j]
                outs[j].append(_dot_nt(w[4][...], hid[:, gg * CMP_HIDDEN:(gg + 1) * CMP_HIDDEN]))
    return outs


def _compress_kernel(n_tiles, kt_ref, perm_ref, wab_ref, pe2_ref, w1f_ref, w2t_ref, o_ref, kp_ref, sh_ref):
    tile = lambda pair, t: kt_ref[pair * LANE:(pair + 1) * LANE, t * LANE:(t + 1) * LANE]
    (outs,) = _compress_core([(tile, kp_ref, sh_ref, (perm_ref, wab_ref, pe2_ref, w1f_ref, w2t_ref))], n_tiles)
    for g in range(KV_HEADS):
        o_ref[g] = outs[g]


def _compress_scratch(n_chunks):
    return [pltpu.VMEM(((KV_HEADS // 2) * n_chunks, CMP_STRIDE * LANE), F32),
            pltpu.VMEM((KV_HEADS // 2, n_chunks + 8, 2 * CMP_HIDDEN), F32)]


def _compress_prompt(kt, cw):
    n, _, length = kt.shape
    n_chunks = length // CMP_STRIDE
    const = lambda a: pl.BlockSpec(a.shape, lambda i: (0,) * a.ndim)
    return pl.pallas_call(
        functools.partial(_compress_kernel, length // LANE),
        grid=(n,),
        in_specs=[pl.BlockSpec((None, KV_WIDTH, length), lambda i: (i, 0, 0))] + [const(a) for a in cw],
        out_specs=pl.BlockSpec((None, KV_HEADS, HEAD_DIM, n_chunks), lambda i: (i, 0, 0, 0)),
        out_shape=jax.ShapeDtypeStruct((n, KV_HEADS, HEAD_DIM, n_chunks), F32),
        scratch_shapes=_compress_scratch(n_chunks),
        compiler_params=_cparams(("parallel",)),
        name="compress_prompt",
    )(kt, *cw)


def _compress_weights(pe, w1, w2):
    eye2 = jnp.eye(2, dtype=F32)
    wa = jnp.einsum("pde,ab->padbe", w1[:CMP_STRIDE], eye2).reshape(CMP_STRIDE, LANE, 2 * CMP_HIDDEN)
    wb = jnp.einsum("pde,ab->padbe", w1[CMP_STRIDE:], eye2).reshape(CMP_STRIDE, LANE, 2 * CMP_HIDDEN)
    wab = jnp.concatenate([wa, wb], axis=2).reshape(CMP_STRIDE * LANE, 4 * CMP_HIDDEN)
    pe2 = jnp.zeros((8, CMP_LEN * HEAD_DIM), F32).at[0].set(pe.reshape(-1))
    w1f = w1.reshape(CMP_LEN * HEAD_DIM, CMP_HIDDEN)
    per_tile = LANE // CMP_STRIDE
    perm = np.zeros((LANE, LANE), np.float32)
    for p in range(CMP_STRIDE):
        for i in range(per_tile):
            perm[p * per_tile + i, CMP_STRIDE * i + p] = 1.0
    return (jnp.asarray(perm, dtype=BF16), wab.astype(BF16), pe2, w1f.astype(BF16), w2.T.astype(BF16))


def _overlap_matrix(n_cmp, n_blk, rows, cols):
    c0 = np.arange(n_cmp)[:, None] * CMP_STRIDE
    b0 = np.arange(n_blk)[None, :] * SLC_BLOCK
    inter = np.minimum(c0 + CMP_LEN, b0 + SLC_BLOCK) - np.maximum(c0, b0)
    ov = np.zeros((rows, cols), np.float32)
    ov[:n_cmp, :n_blk] = np.clip(inter, 0, None) / CMP_LEN
    return jnp.asarray(ov)


def _topk_select(imp, n_blk, top_n):
    lane = lax.broadcasted_iota(jnp.int32, imp.shape, 1)
    rank = jnp.zeros(imp.shape, F32)
    for i in range(n_blk):
        col = imp[:, i:i + 1]
        ahead = (col > imp) | ((col == imp) & (lane > i))
        rank = rank + ahead.astype(F32)
    return (rank < float(top_n)).astype(F32)


def _imp_matmul(psum, ov):
    hi = psum.astype(BF16)
    lo = (psum - hi.astype(F32)).astype(BF16)
    ovb = ov.astype(BF16)
    return jnp.dot(hi, ovb, preferred_element_type=F32) + jnp.dot(lo, ovb, preferred_element_type=F32)


NSA_TQ = 256
NSA_TK_SLC = 1024
LOG2E = 1.4426950408889634


def _nsa_prompt_kernel(seq, q_ref, gate_ref, kc_ref, vc_ref, ks_ref, vs_ref, kw_ref, vw_ref, ovt_ref, eb_ref,
                       o_ref):
    tq = NSA_TQ
    g = pl.program_id(1)
    i = pl.program_id(2)
    s0 = i * tq
    n_cmp = seq // CMP_STRIDE - 1
    n_blk = seq // SLC_BLOCK
    top_n = min(SLC_TOPN, n_blk)

    q = q_ref[...].astype(F32) * (HEAD_DIM ** -0.5)
    qh = [q[:, h * HEAD_DIM:(h + 1) * HEAD_DIM].astype(BF16) for h in range(GROUP)]
    q4 = jnp.concatenate(qh, axis=0)
    qpos = s0 + lax.broadcasted_iota(jnp.int32, (tq, 1), 0)

    n_c = kc_ref.shape[1]
    s = _dot(q4, kc_ref[...]).reshape(GROUP, tq, n_c)
    cidx = lax.broadcasted_iota(jnp.int32, (tq, n_c), 1)
    valid = ((cidx * CMP_STRIDE + (CMP_LEN - 1)) <= qpos) & (cidx < n_cmp)
    sm = jnp.where(valid[None], s, NEG)
    m = jnp.max(sm, axis=-1, keepdims=True)
    p = jnp.where(valid[None], jnp.exp(sm - m), 0.0)
    p = p / jnp.maximum(jnp.sum(p, axis=-1, keepdims=True), 1e-30)
    o_cmp = _dot_nt(p.reshape(GROUP * tq, n_c), vc_ref[...])
    psum = p[0] + p[1] + p[2] + p[3]

    hi = psum.astype(BF16)
    lo = (psum - hi.astype(F32)).astype(BF16)
    imp_t = _dot_nt(ovt_ref[...], hi) + _dot_nt(ovt_ref[...], lo)
    nb8 = -(-n_blk // 8) * 8
    imp_t = imp_t[:nb8]
    blk = lax.broadcasted_iota(jnp.int32, (nb8, tq), 0)
    qrow = s0 + lax.broadcasted_iota(jnp.int32, (nb8, tq), 1)
    forced = (blk == qrow // SLC_BLOCK) | (blk == 0)
    causal = ((blk * SLC_BLOCK) <= qrow) & (blk < n_blk)
    imp_t = jnp.where(forced, jnp.inf, jnp.where(causal, imp_t, -jnp.inf))
    rank = jnp.zeros((nb8, tq), F32)
    for j in range(n_blk):
        row = imp_t[j:j + 1, :]
        ahead = (row > imp_t) | ((row == imp_t) & (blk > j))
        rank = rank + ahead.astype(F32)
    half = LANE // 2
    sel_t = jnp.where((rank < float(top_n)) & causal, 0.0, NEG)
    if nb8 < half:
        sel_t = jnp.concatenate([sel_t, jnp.full((half - nb8, tq), NEG, F32)], axis=0)
    selneg = jnp.concatenate([sel_t, sel_t], axis=0).T
    lane_q = lax.broadcasted_iota(jnp.int32, (tq, LANE), 1)
    q2 = q_ref[...].astype(F32) * (HEAD_DIM ** -0.5 * LOG2E)
    q_aug = []
    for h in range(GROUP):
        tile = q2[:, (h // 2) * LANE:(h // 2 + 1) * LANE]
        own = (lane_q < half) if h % 2 == 0 else (lane_q >= half)
        q_aug.append(jnp.where(own, tile, selneg).astype(BF16))
    tk = NSA_TK_SLC
    kt_d = (s0 + tq - 1) // tk

    def slc_tile(kt, carry, bias):
        k0 = pl.multiple_of(kt * tk, tk)
        kT = ks_ref[:, pl.ds(k0, tk)].astype(BF16)
        eT = eb_ref[:, pl.ds(k0, tk)]
        k_aug = (jnp.concatenate([kT, eT], axis=0), jnp.concatenate([eT, kT], axis=0))
        vT = vs_ref[:, pl.ds(k0, tk)].astype(BF16)
        out = []
        for h in range(GROUP):
            m, l, acc = carry[h]
            sm = jnp.dot(q_aug[h], k_aug[h % 2], preferred_element_type=F32)
            if bias is not None:
                sm = sm + bias
            m_new = jnp.maximum(m, jnp.max(sm, axis=-1, keepdims=True))
            alpha = jnp.exp2(m - m_new)
            p = jnp.exp2(sm - m_new)
            l = alpha * l + jnp.sum(p, axis=-1, keepdims=True)
            acc = alpha * acc + _dot_nt(p, vT)
            out.append((m_new, l, acc))
        return tuple(out)

    init = tuple((jnp.full((tq, 1), NEG, F32), jnp.zeros((tq, 1), F32), jnp.zeros((tq, HEAD_DIM), F32))
                 for _ in range(GROUP))
    kpos_d = kt_d * tk + lax.broadcasted_iota(jnp.int32, (tq, tk), 1)
    carry = slc_tile(kt_d, init, jnp.where(kpos_d <= qpos, 0.0, NEG))
    carry = lax.fori_loop(0, kt_d, lambda j, c: slc_tile(kt_d - 1 - j, c, None), carry)
    o_slc = jnp.concatenate([acc / l for _, l, acc in carry], axis=0)

    span = min(WINDOW + tq, seq)
    w0 = pl.multiple_of(jnp.minimum(jnp.maximum(s0 - WINDOW, 0), seq - span), tq)
    kT = kw_ref[:, pl.ds(w0, span)].astype(BF16)
    vT = vw_ref[:, pl.ds(w0, span)].astype(BF16)
    kpos = w0 + lax.broadcasted_iota(jnp.int32, (tq, span), 1)
    bias = jnp.where((kpos <= qpos) & (kpos > qpos - WINDOW), 0.0, NEG)
    o_win = []
    for h in range(GROUP):
        sm = _dot(qh[h], kT) + bias
        p = jnp.exp(sm - jnp.max(sm, axis=-1, keepdims=True))
        o_win.append(_dot_nt(p, vT) / jnp.sum(p, axis=-1, keepdims=True))
    o_win = jnp.concatenate(o_win, axis=0)

    sg = _sigmoid(gate_ref[...])
    lane = lax.broadcasted_iota(jnp.int32, (tq, LANE), 1)
    for h in range(GROUP):
        base = g * (GROUP * 3) + h * 3
        gc = [jnp.sum(jnp.where(lane == base + b, sg, 0.0), axis=-1, keepdims=True) for b in range(3)]
        rows = slice(h * tq, (h + 1) * tq)
        out = gc[0] * o_cmp[rows] + gc[1] * o_slc[rows] + gc[2] * o_win[rows]
        o_ref[:, h * HEAD_DIM:(h + 1) * HEAD_DIM] = out


def _nsa_prompt(q, small, kct, vct, kst, vst, kwt, vwt):
    n, seq, _ = q.shape
    n_c = kct.shape[-1]
    assert NSA_TK_SLC % NSA_TQ == 0 and seq % NSA_TK_SLC == 0 and WINDOW % NSA_TQ == 0
    assert seq // SLC_BLOCK <= LANE // 2
    ov = _overlap_matrix(seq // CMP_STRIDE - 1, seq // SLC_BLOCK, n_c, LANE).T
    eb = np.zeros((LANE // 2, seq), np.float32)
    eb[np.arange(seq) // SLC_BLOCK, np.arange(seq)] = 1.0
    eb = jnp.asarray(eb, dtype=BF16)
    cw = GROUP * HEAD_DIM
    head_spec = lambda width: pl.BlockSpec((None, None, HEAD_DIM, width), lambda b, g, i: (b, g, 0, 0))
    r4 = lambda a: a.reshape(n, KV_HEADS, HEAD_DIM, seq)
    return pl.pallas_call(
        functools.partial(_nsa_prompt_kernel, seq),
        grid=(n, KV_HEADS, seq // NSA_TQ),
        in_specs=[pl.BlockSpec((None, NSA_TQ, cw), lambda b, g, i: (b, i, g)),
                  pl.BlockSpec((None, NSA_TQ, LANE), lambda b, g, i: (b, i, 0)),
                  head_spec(n_c), head_spec(n_c), head_spec(seq), head_spec(seq), head_spec(seq), head_spec(seq),
                  pl.BlockSpec(ov.shape, lambda b, g, i: (0, 0)),
                  pl.BlockSpec(eb.shape, lambda b, g, i: (0, 0))],
        out_specs=pl.BlockSpec((None, NSA_TQ, cw), lambda b, g, i: (b, i, g)),
        out_shape=jax.ShapeDtypeStruct((n, seq, D_MODEL), F32),
        compiler_params=_cparams(("parallel", "parallel", "parallel")),
        name="nsa_prompt",
    )(q, small, kct, vct, r4(kst), r4(vst), r4(kwt), r4(vwt), ov, eb)


DN_C = 128


def _dn_gates(small, alog_ref, dt_ref):
    g = -jnp.exp(alog_ref[...]) * _softplus(small + dt_ref[...])
    return g, _sigmoid(small)


def _l2n(x):
    return x * lax.rsqrt(jnp.sum(x * x, axis=-1, keepdims=True) + EPS)


def _dn_out(o, z, ng):
    on = o * lax.rsqrt(jnp.mean(o * o, axis=-1, keepdims=True) + EPS) * ng
    return on * (z * _sigmoid(z))


INV_BASE = 16


def _unit_lower_inverse(lmats, eye, ri, ci, c):
    same = (ri // INV_BASE) == (ci // INV_BASE)
    pws = [jnp.where(same, -l, 0.0) for l in lmats]
    xs = [eye + pw for pw in pws]
    pws = [_dot(pw, pw) for pw in pws]
    steps = int(np.log2(INV_BASE)) - 1
    for k in range(steps):
        if k < steps - 1:
            rs = [_dot(pw, jnp.concatenate([pw, x], axis=1)) for pw, x in zip(pws, xs)]
            xs = [x + r[:, c:] for x, r in zip(xs, rs)]
            pws = [r[:, :c] for r in rs]
        else:
            xs = [x + _dot(pw, x) for x, pw in zip(xs, pws)]
    s = INV_BASE
    while s < c:
        off = ((ri // (2 * s)) == (ci // (2 * s))) & ((ri // s) != (ci // s))
        ts = [_dot(x, jnp.where(off, l, 0.0)) for x, l in zip(xs, lmats)]
        xs = [x - _dot(t, x) for x, t in zip(xs, ts)]
        s *= 2
    return xs


def _dn_prompt_kernel(qkv_ref, z_ref, small_ref, cw_ref, alog_ref, dt_ref, ng_ref, o_ref, st_ref, xbuf, s_ref):
    c = DN_C
    i = pl.program_id(1)
    last = pl.num_programs(1) - 1

    @pl.when(i == 0)
    def _():
        xbuf[0:8, :] = jnp.zeros((8, DN_QKV), F32)
        s_ref[...] = jnp.zeros_like(s_ref)

    xbuf[8:8 + c, :] = qkv_ref[...]
    y = (cw_ref[3:4, :] * xbuf[8:8 + c, :] + cw_ref[2:3, :] * xbuf[7:7 + c, :]
         + cw_ref[1:2, :] * xbuf[6:6 + c, :] + cw_ref[0:1, :] * xbuf[5:5 + c, :])
    xbuf[0:8, :] = xbuf[c:c + 8, :]
    y = y * _sigmoid(y)

    gl, bt = _dn_gates(small_ref[...], alog_ref, dt_ref)
    ri = lax.broadcasted_iota(jnp.int32, (c, c), 0)
    ci = lax.broadcasted_iota(jnp.int32, (c, c), 1)
    incl = ri >= ci
    strict = ri > ci
    gc_col = _dot_exact_lhs(incl.astype(F32), gl)
    gc_row = _dot_exact_rhs(gl.T, (ri <= ci).astype(F32))
    eye = (ri == ci).astype(F32)
    nd = DN_HEADS * DN_DK

    hs = range(DN_HEADS)
    sls = [slice(h * DN_DK, (h + 1) * DN_DK) for h in hs]
    q = [_l2n(y[:, sl]) * (DN_DK ** -0.5) for sl in sls]
    k = [_l2n(y[:, nd + h * DN_DK:nd + (h + 1) * DN_DK]) for h in hs]
    v = [y[:, 2 * nd + h * DN_DK:2 * nd + (h + 1) * DN_DK] for h in hs]
    gcc = [gc_col[:, A_LANE + h:A_LANE + h + 1] for h in hs]
    gcr = [gc_row[A_LANE + h:A_LANE + h + 1, :] for h in hs]
    beta = [bt[:, B_LANE + h:B_LANE + h + 1] for h in hs]
    decay = [jnp.exp(jnp.where(incl, gcc[h] - gcr[h], NEG)) for h in hs]
    kb = [k[h] * beta[h] for h in hs]
    kq = [_dot_nt(jnp.concatenate([kb[h], q[h]], axis=0), k[h]) for h in hs]
    lmat = [jnp.where(strict, kq[h][:c] * decay[h], 0.0) for h in hs]
    a_intra = [kq[h][c:] * decay[h] for h in hs]
    tmat = _unit_lower_inverse(lmat, eye, ri, ci, c)
    eg = [jnp.exp(gcc[h]) for h in hs]
    uw = [_dot(tmat[h], jnp.concatenate([v[h] * beta[h], kb[h] * eg[h]], axis=1)) for h in hs]
    g_last = [gcr[h][:, c - 1:c] for h in hs]
    k_dec_t = [(k[h] * jnp.exp(g_last[h] - gcc[h])).T for h in hs]
    state = [s_ref[h] for h in hs]
    ws = [_dot(jnp.concatenate([uw[h][:, DN_DK:], q[h] * eg[h]], axis=0), state[h]) for h in hs]
    v_new = [uw[h][:, :DN_DK] - ws[h][:c] for h in hs]
    o = [ws[h][c:] + _dot(a_intra[h], v_new[h]) for h in hs]
    for h in hs:
        s_ref[h] = state[h] * jnp.exp(g_last[h]) + _dot(k_dec_t[h], v_new[h])
    for h in hs:
        o_ref[:, sls[h]] = _dn_out(o[h], z_ref[:, sls[h]], ng_ref[...])

    @pl.when(i == last)
    def _():
        st_ref[...] = s_ref[...]


def _dn_prompt(qkv, z, small, conv_w, alog_row, dt_row, norm_g):
    n, seq, _ = qkv.shape
    c = DN_C
    const = lambda a: pl.BlockSpec(a.shape, lambda b, i: (0,) * a.ndim)
    return pl.pallas_call(
        _dn_prompt_kernel,
        grid=(n, seq // c),
        in_specs=[pl.BlockSpec((None, c, DN_QKV), lambda b, i: (b, i, 0)),
                  pl.BlockSpec((None, c, D_MODEL), lambda b, i: (b, i, 0)),
                  pl.BlockSpec((None, c, LANE), lambda b, i: (b, i, 0)),
                  const(conv_w), const(alog_row), const(dt_row), const(norm_g)],
        out_specs=[pl.BlockSpec((None, c, D_MODEL), lambda b, i: (b, i, 0)),
                   pl.BlockSpec((None, DN_HEADS, DN_DK, DN_DK), lambda b, i: (b, 0, 0, 0))],
        out_shape=[jax.ShapeDtypeStruct((n, seq, D_MODEL), F32),
                   jax.ShapeDtypeStruct((n, DN_HEADS, DN_DK, DN_DK), F32)],
        scratch_shapes=[pltpu.VMEM((c + 8, DN_QKV), F32), pltpu.VMEM((DN_HEADS, DN_DK, DN_DK), F32)],
        compiler_params=_cparams(("parallel", "arbitrary")),
        name="dn_prompt",
    )(qkv, z, small, conv_w, alog_row, dt_row, norm_g)


MLP_TF = 1024


def _mlp_kernel(x_ref, on_ref, od_ref, mg_ref, mod_ref, g2_ref, gf_ref, wo_ref, wu_ref, wd_ref, y_ref,
                x1_ref, h2_ref, acc_ref):
    j = pl.program_id(2)
    last = pl.num_programs(2) - 1

    @pl.when(j == 0)
    def _():
        ga = _sigmoid(mg_ref[:, :D_MODEL])
        gb = _sigmoid(mg_ref[:, D_MODEL:])
        mixed = ga * on_ref[...] + gb * od_ref[...]
        x1 = x_ref[...] + mod_ref[2] * _dot(mixed, wo_ref[...])
        x1_ref[...] = x1
        h2_ref[...] = _norm_mod(x1, g2_ref[...], mod_ref[4], mod_ref[3]).astype(BF16)
        acc_ref[...] = jnp.zeros_like(acc_ref)

    up = jnp.dot(h2_ref[...], wu_ref[...], preferred_element_type=F32)
    act = jnp.square(jnp.maximum(up, 0.0))
    acc_ref[...] += _dot(act, wd_ref[...])

    @pl.when(j == last)
    def _():
        x2 = x1_ref[...] + mod_ref[5] * acc_ref[...]
        y_ref[...] = x2 * lax.rsqrt(jnp.mean(x2 * x2, axis=-1, keepdims=True) + EPS) * gf_ref[...]


def _mlp(x, o_nsa, o_dn, merge, mod, g2, gf, w_out, w_up, w_down, tm, name):
    G, R, _ = x.shape
    rm = mod.shape[2]
    tf = MLP_TF
    row = lambda width: pl.BlockSpec((None, tm, width), lambda n, i, j: (n, i, 0))
    return pl.pallas_call(
        _mlp_kernel,
        grid=(G, R // tm, D_FF // tf),
        in_specs=[row(D_MODEL), row(D_MODEL), row(D_MODEL), row(2 * D_MODEL),
                  pl.BlockSpec((None, 6, rm, D_MODEL), lambda n, i, j: (n, 0, 0, 0)),
                  pl.BlockSpec((1, D_MODEL), lambda n, i, j: (0, 0)),
                  pl.BlockSpec((1, D_MODEL), lambda n, i, j: (0, 0)),
                  pl.BlockSpec((D_MODEL, D_MODEL), lambda n, i, j: (0, 0)),
                  pl.BlockSpec((D_MODEL, tf), lambda n, i, j: (0, j)),
                  pl.BlockSpec((tf, D_MODEL), lambda n, i, j: (j, 0))],
        out_specs=row(D_MODEL),
        out_shape=jax.ShapeDtypeStruct((G, R, D_MODEL), F32),
        scratch_shapes=[pltpu.VMEM((tm, D_MODEL), F32), pltpu.VMEM((tm, D_MODEL), BF16),
                        pltpu.VMEM((tm, D_MODEL), F32)],
        compiler_params=_cparams(("parallel", "parallel", "arbitrary")),
        name=name,
    )(x, o_nsa, o_dn, merge, mod, g2, gf, w_out, w_up, w_down)


def _nsa_sample_kernel(n_pages, page, win, *refs):
    n_in = 4 * n_pages
    pt_ref = refs[0]
    del pt_ref
    pools = [refs[1 + t * n_pages:1 + (t + 1) * n_pages] for t in range(4)]
    (q_ref, gate_ref, newt_ref, kwc_ref, vwc_ref, ov_ref, ex_ref) = refs[1 + n_in:1 + n_in + 7]
    rows16 = newt_ref[...]
    new_t = jnp.concatenate([rows16, jnp.zeros((LANE - rows16.shape[0], LANE), F32)], axis=0).T
    new_ref = [jnp.concatenate([new_t[:, 2 * t:2 * t + 1], new_t[:, 2 * t + 1:2 * t + 2]], axis=0)
               for t in range(6)]
    cwk = refs[1 + n_in + 7:1 + n_in + 12]
    cwv = refs[1 + n_in + 12:1 + n_in + 17]
    o_ref, kwo_ref, vwo_ref = refs[1 + n_in + 17:1 + n_in + 20]
    full_a, full_b, kpk_ref, shk_ref, kpv_ref, shv_ref = refs[1 + n_in + 20:]

    past = n_pages * page
    total = past + 1
    lp = -(-total // SLC_BLOCK) * SLC_BLOCK
    n_cmp = lp // CMP_STRIDE - 1
    n_blk = lp // SLC_BLOCK
    top_n = min(SLC_TOPN, n_blk)
    length = full_a.shape[1]
    n_c = length // CMP_STRIDE
    n_tiles = length // LANE

    def fill(full_ref, page_refs, t):
        for j in range(n_pages):
            full_ref[:, j * page:(j + 1) * page] = page_refs[j][...]
        full_ref[:, past:length] = jnp.zeros((KV_WIDTH, length - past), F32)
        full_ref[:, past:past + 1] = new_ref[t]

    def cmp_tiles(page_refs, t):
        def tile(pair, j):
            rows = slice(pair * LANE, (pair + 1) * LANE)
            if j < n_pages:
                return page_refs[j][rows, :]
            if j == n_pages:
                lane = lax.broadcasted_iota(jnp.int32, (LANE, LANE), 1)
                return jnp.where(lane == 0, new_ref[t][rows, :], 0.0)
            return jnp.zeros((LANE, LANE), F32)
        return tile

    q16 = (q_ref[...].astype(F32) * (HEAD_DIM ** -0.5)).astype(BF16)
    head_group = lax.broadcasted_iota(jnp.int32, (N_HEADS, 1), 0) // GROUP

    def by_group(vals):
        out = vals[KV_HEADS - 1]
        for g in range(KV_HEADS - 2, -1, -1):
            out = jnp.where(head_group == g, vals[g], out)
        return out

    def attend(kts, vts, valid):
        s = by_group([_dot(q16, kt) for kt in kts])
        sm = jnp.where(valid, s, NEG)
        m = jnp.max(sm, axis=-1, keepdims=True)
        p = jnp.where(valid, jnp.exp(sm - m), 0.0)
        p = p / jnp.maximum(jnp.sum(p, axis=-1, keepdims=True), 1e-30)
        o = by_group([_dot_nt(p, vt) for vt in vts])
        return p, o

    lane_w = lax.broadcasted_iota(jnp.int32, (KV_WIDTH, win), 1)
    kw = jnp.where(lane_w == win - 1, new_ref[4], pltpu.roll(kwc_ref[...], win - 1, axis=1))
    vw = jnp.where(lane_w == win - 1, new_ref[5], pltpu.roll(vwc_ref[...], win - 1, axis=1))
    kwo_ref[...] = kw
    vwo_ref[...] = vw
    wpos = past - win + 1 + lax.broadcasted_iota(jnp.int32, (1, win), 1)
    valid_w = (wpos <= past) & (wpos > past - WINDOW) & (wpos >= 0)
    kts = [kw[g * HEAD_DIM:(g + 1) * HEAD_DIM, :] for g in range(KV_HEADS)]
    vts = [vw[g * HEAD_DIM:(g + 1) * HEAD_DIM, :] for g in range(KV_HEADS)]
    _, o_win = attend(kts, vts, valid_w)
    fill(full_a, pools[2], 2)
    fill(full_b, pools[3], 3)

    kcs, vcs = _compress_core([(cmp_tiles(pools[0], 0), kpk_ref, shk_ref, cwk),
                               (cmp_tiles(pools[1], 1), kpv_ref, shv_ref, cwv)], n_tiles)
    cidx = lax.broadcasted_iota(jnp.int32, (1, n_c), 1)
    valid_c = ((cidx * CMP_STRIDE + (CMP_LEN - 1)) <= past) & (cidx < n_cmp)
    p_cmp, o_cmp = attend(kcs, vcs, valid_c)
    imp_h = _imp_matmul(p_cmp, ov_ref[...])
    hr = lax.broadcasted_iota(jnp.int32, (N_HEADS, N_HEADS), 0) // GROUP
    hc = lax.broadcasted_iota(jnp.int32, (N_HEADS, N_HEADS), 1) // GROUP
    imp = _dot_exact_lhs((hr == hc).astype(F32), imp_h)
    blk = lax.broadcasted_iota(jnp.int32, (N_HEADS, LANE), 1)
    forced = (blk == past // SLC_BLOCK) | (blk == 0)
    causal = (blk * SLC_BLOCK) <= past
    imp = jnp.where(forced, jnp.inf, jnp.where(causal & (blk < n_blk), imp, -jnp.inf))
    sel = _topk_select(imp, n_blk, top_n).astype(BF16)

    kpos = lax.broadcasted_iota(jnp.int32, (1, length), 1)
    chosen = jnp.dot(sel, ex_ref[...], preferred_element_type=F32) > 0.5
    valid_s = chosen & (kpos <= past)
    kts = [full_a[g * HEAD_DIM:(g + 1) * HEAD_DIM, :] for g in range(KV_HEADS)]
    vts = [full_b[g * HEAD_DIM:(g + 1) * HEAD_DIM, :] for g in range(KV_HEADS)]
    _, o_slc = attend(kts, vts, valid_s)

    sg = _sigmoid(gate_ref[...])
    o_ref[...] = sg[:, 0:1] * o_cmp + sg[:, 1:2] * o_slc + sg[:, 2:3] * o_win


def _nsa_sample(page_table, pools, q16, gates, newcols, kwc, vwc, cwk, cwv):
    n_seq, n_pages = page_table.shape
    page = pools[0].shape[-1]
    win = kwc.shape[-1]
    past = n_pages * page
    lp = -(-(past + 1) // SLC_BLOCK) * SLC_BLOCK
    length = -(-lp // LANE) * LANE
    n_c = length // CMP_STRIDE
    ov = _overlap_matrix(lp // CMP_STRIDE - 1, lp // SLC_BLOCK, n_c, LANE)
    ex = np.zeros((LANE, length), np.float32)
    ex[np.arange(length) // SLC_BLOCK, np.arange(length)] = 1.0
    ex = jnp.asarray(ex, dtype=BF16)
    assert win == WINDOW and page == LANE

    in_specs = []
    for _ in range(4):
        for j in range(n_pages):
            in_specs.append(pl.BlockSpec((None, KV_WIDTH, page), functools.partial(
                lambda s, pt, jj: (pt[s, jj], 0, 0), jj=j)))
    const = lambda a: pl.BlockSpec(a.shape, lambda s, pt: (0,) * a.ndim)
    in_specs += [pl.BlockSpec((None, N_HEADS, HEAD_DIM), lambda s, pt: (s, 0, 0)),
                 pl.BlockSpec((None, N_HEADS, 3), lambda s, pt: (s, 0, 0)),
                 pl.BlockSpec((None,) + newcols.shape[1:], lambda s, pt: (s, 0, 0)),
                 pl.BlockSpec((None, KV_WIDTH, win), lambda s, pt: (s, 0, 0)),
                 pl.BlockSpec((None, KV_WIDTH, win), lambda s, pt: (s, 0, 0)),
                 const(ov), const(ex)] + [const(a) for a in cwk] + [const(a) for a in cwv]
    operands = []
    for t in range(4):
        operands += [pools[t]] * n_pages
    operands += [q16, gates, newcols, kwc, vwc, ov, ex, *cwk, *cwv]
    win_spec = pl.BlockSpec((None, KV_WIDTH, win), lambda s, pt: (s, 0, 0))
    grid_spec = pltpu.PrefetchScalarGridSpec(
        num_scalar_prefetch=1, grid=(n_seq,), in_specs=in_specs,
        out_specs=[pl.BlockSpec((None, N_HEADS, HEAD_DIM), lambda s, pt: (s, 0, 0)), win_spec, win_spec],
        scratch_shapes=[pltpu.VMEM((KV_WIDTH, length), F32), pltpu.VMEM((KV_WIDTH, length), F32)]
        + _compress_scratch(n_c) + _compress_scratch(n_c))
    return pl.pallas_call(
        functools.partial(_nsa_sample_kernel, n_pages, page, win),
        grid_spec=grid_spec,
        out_shape=[jax.ShapeDtypeStruct((n_seq, N_HEADS, HEAD_DIM), F32),
                   jax.ShapeDtypeStruct((n_seq, KV_WIDTH, win), F32),
                   jax.ShapeDtypeStruct((n_seq, KV_WIDTH, win), F32)],
        compiler_params=_cparams(("arbitrary",)),
        name="nsa_sample",
    )(page_table, *operands)


DN_SB = 8


def _dn_sample_kernel(qkv_ref, conv_ref, z_ref, small_ref, st_ref, cw_ref, alog_ref, dt_ref, ng_ref,
                      o_ref, conv_o_ref, st_o_ref):
    x = qkv_ref[...]
    y = (cw_ref[0:1, :] * conv_ref[0] + cw_ref[1:2, :] * conv_ref[1] + cw_ref[2:3, :] * conv_ref[2]
         + cw_ref[3:4, :] * x)
    conv_o_ref[0] = conv_ref[1]
    conv_o_ref[1] = conv_ref[2]
    conv_o_ref[2] = x
    y = y * _sigmoid(y)
    gl, bt = _dn_gates(small_ref[...], alog_ref, dt_ref)
    nd = DN_HEADS * DN_DK
    ri = lax.broadcasted_iota(jnp.int32, (DN_DK, DN_DK), 0)
    ci = lax.broadcasted_iota(jnp.int32, (DN_DK, DN_DK), 1)
    eye = ri == ci
    for h in range(DN_HEADS):
        sl = slice(h * DN_DK, (h + 1) * DN_DK)
        q = _l2n(y[:, sl]) * (DN_DK ** -0.5)
        k = _l2n(y[:, nd + h * DN_DK:nd + (h + 1) * DN_DK])
        v = y[:, 2 * nd + h * DN_DK:2 * nd + (h + 1) * DN_DK]
        g = gl[:, A_LANE + h:A_LANE + h + 1]
        beta = bt[:, B_LANE + h:B_LANE + h + 1]
        eg = jnp.exp(g)
        u = v * beta
        w = k * beta * eg
        a_intra = jnp.sum(q * k, axis=-1, keepdims=True)
        q_dec = q * eg
        z = z_ref[:, sl]
        for b in range(DN_SB):
            state = st_ref[b, h]
            lhs = jnp.concatenate([w[b:b + 1], q_dec[b:b + 1], jnp.zeros((6, DN_DK), F32)], axis=0)
            res = _dot(lhs, state)
            v_new = u[b:b + 1] - res[0:1]
            o = res[1:2] + a_intra[b:b + 1] * v_new
            k_col = jnp.sum(jnp.where(eye, k[b:b + 1], 0.0), axis=-1, keepdims=True)
            st_o_ref[b, h] = state * eg[b:b + 1] + k_col * v_new
            o_ref[b:b + 1, sl] = _dn_out(o, z[b:b + 1], ng_ref[...])


def _dn_sample(qkv, conv_t, z, small, state, conv_w, alog_row, dt_row, norm_g):
    n_seq = qkv.shape[0]
    sb = DN_SB
    const = lambda a: pl.BlockSpec(a.shape, lambda i: (0,) * a.ndim)
    return pl.pallas_call(
        _dn_sample_kernel,
        grid=(n_seq // sb,),
        in_specs=[pl.BlockSpec((sb, DN_QKV), lambda i: (i, 0)),
                  pl.BlockSpec((CONV_W - 1, sb, DN_QKV), lambda i: (0, i, 0)),
                  pl.BlockSpec((sb, D_MODEL), lambda i: (i, 0)),
                  pl.BlockSpec((sb, LANE), lambda i: (i, 0)),
                  pl.BlockSpec((sb, DN_HEADS, DN_DK, DN_DK), lambda i: (i, 0, 0, 0)),
                  const(conv_w), const(alog_row), const(dt_row), const(norm_g)],
        out_specs=[pl.BlockSpec((sb, D_MODEL), lambda i: (i, 0)),
                   pl.BlockSpec((CONV_W - 1, sb, DN_QKV), lambda i: (0, i, 0)),
                   pl.BlockSpec((sb, DN_HEADS, DN_DK, DN_DK), lambda i: (i, 0, 0, 0))],
        out_shape=[jax.ShapeDtypeStruct((n_seq, D_MODEL), F32),
                   jax.ShapeDtypeStruct((CONV_W - 1, n_seq, DN_QKV), F32),
                   jax.ShapeDtypeStruct(state.shape, F32)],
        compiler_params=_cparams(("parallel",)),
        name="dn_sample",
    )(qkv, conv_t, z, small, state, conv_w, alog_row, dt_row, norm_g)


def _pick_tile(rows, pref):
    t = min(rows, pref)
    while rows % t:
        t //= 2
    return t


def _layer_weights(w_in):
    w = w_in.astype(BF16)
    o = 0
    wq = w[:, o:o + D_MODEL]; o += D_MODEL
    wkv = [w[:, o + t * KV_WIDTH:o + (t + 1) * KV_WIDTH].T for t in range(6)]; o += 6 * KV_WIDTH
    wg = w[:, o:o + 3 * N_HEADS]; o += 3 * N_HEADS
    wqkv = w[:, o:o + DN_QKV]; o += DN_QKV
    wz = w[:, o:o + D_MODEL]; o += D_MODEL
    wa = w[:, o:o + DN_HEADS]; o += DN_HEADS
    wb = w[:, o:o + DN_HEADS]; o += DN_HEADS
    wm = w[:, o:o + 2 * D_MODEL]
    wsmall = jnp.concatenate([wg, wa, wb, jnp.zeros((D_MODEL, LANE - B_LANE - DN_HEADS), BF16)], axis=1)
    return wq, wkv, wqkv, wz, wm, wsmall


def kernel(x_prompt, x_sample, c_prompt, c_sample, cache_k_cmp, cache_v_cmp, cache_k_slc, cache_v_slc, cache_k_win, cache_v_win, state_conv, state_dn, page_table, w_ada, b_ada, norm1_g, norm2_g, w_in, cmp_pe_k, cmp_w1_k, cmp_w2_k, cmp_pe_v, cmp_w1_v, cmp_w2_v, dn_conv_w, dn_a_log, dn_dt_bias, dn_norm_g, w_out, w_up, w_down, final_g):
    assert w_ada.shape[0] == 1, "single-layer stack"
    nb, seq, _ = x_prompt.shape
    ns = x_sample.shape[0]
    assert x_sample.shape[1] == 1

    n_c = nb + ns
    rows = -(-n_c // 8) * 8
    c_all = jnp.concatenate([c_prompt, c_sample, jnp.zeros((rows - n_c, D_MODEL), F32)], axis=0)
    mod = _ada(c_all, w_ada[0], b_ada)
    mod_p = mod[:nb].reshape(nb, 6, 1, D_MODEL)
    mod_s = mod[nb:n_c].reshape(ns, 6, D_MODEL).transpose(1, 0, 2)[None]

    wq, wkv, wqkv, wz, wm, wsmall = _layer_weights(w_in[0])
    g1 = norm1_g
    g2 = norm2_g
    gf = final_g.reshape(1, D_MODEL)
    alog_row = jnp.zeros((1, LANE), F32).at[0, A_LANE:A_LANE + DN_HEADS].set(dn_a_log[0])
    dt_row = jnp.zeros((1, LANE), F32).at[0, A_LANE:A_LANE + DN_HEADS].set(dn_dt_bias[0])
    conv_w = dn_conv_w[0]
    cwk = _compress_weights(cmp_pe_k[0], cmp_w1_k[0], cmp_w2_k[0])
    cwv = _compress_weights(cmp_pe_v[0], cmp_w1_v[0], cmp_w2_v[0])
    wo = w_out[0].astype(BF16)
    wu = w_up[0].astype(BF16)
    wd = w_down[0].astype(BF16)

    def project(x, modg, tm, tag):
        q, *kvt = _proj(x, modg, g1, [wq] + wkv, ["nn"] + ["nt"] * 6, [BF16] + [F32] * 6, tm, "proj_a_" + tag)
        (qkv,) = _proj(x, modg, g1, [wqkv], ["nn"], [F32], tm, "proj_b_" + tag)
        z, merge, small = _proj(x, modg, g1, [wz, wm, wsmall], ["nn"] * 3, [F32] * 3, tm, "proj_c_" + tag)
        return q, kvt, qkv, z, merge, small

    def kv_out(a):
        n, _, length = a.shape
        return a.reshape(n, KV_HEADS, HEAD_DIM, length).transpose(0, 3, 1, 2)[None]

    tm_p = _pick_tile(seq, 512)
    q, kvt, qkv, z, merge, small = project(x_prompt, mod_p, tm_p, "p")
    kct = _compress_prompt(kvt[0], cwk)
    vct = _compress_prompt(kvt[1], cwv)
    o_nsa = _nsa_prompt(q, small, kct, vct, kvt[2], kvt[3], kvt[4], kvt[5])
    o_dn, p_dn = _dn_prompt(qkv, z, small, conv_w, alog_row, dt_row, dn_norm_g)
    y_prompt = _mlp(x_prompt, o_nsa, o_dn, merge, mod_p, g2, gf, wo, wu, wd, tm_p, "mlp_p")
    wlen = min(WINDOW, seq)
    p_conv = qkv[:, seq - (CONV_W - 1):, :]

    xs = x_sample.reshape(1, ns, D_MODEL)
    qs, kvts, qkvs, zs, merges, smalls = project(xs, mod_s, ns, "s")
    pools = [c[0].transpose(0, 2, 3, 1).reshape(c.shape[1], KV_WIDTH, c.shape[2])
             for c in (cache_k_cmp, cache_v_cmp, cache_k_slc, cache_v_slc)]
    wins = [c[0].transpose(0, 2, 3, 1).reshape(ns, KV_WIDTH, c.shape[2]) for c in (cache_k_win, cache_v_win)]
    newrows = jnp.stack([a[0].T.reshape(ns, KV_WIDTH // LANE, LANE) for a in kvts], axis=1)
    newcols = jnp.pad(newrows.reshape(ns, 6 * (KV_WIDTH // LANE), LANE), ((0, 0), (0, 4), (0, 0)))
    q16 = qs.reshape(ns, N_HEADS, HEAD_DIM)
    gates = smalls[0, :, :3 * N_HEADS].reshape(ns, N_HEADS, 3)
    o_nsa_s, kw_new, vw_new = _nsa_sample(page_table, pools, q16, gates, newcols, wins[0], wins[1], cwk, cwv)
    conv_t = state_conv[0].transpose(1, 0, 2)
    o_dn_s, conv_new, s_dn = _dn_sample(qkvs[0], conv_t, zs[0], smalls[0], state_dn[0], conv_w, alog_row,
                                        dt_row, dn_norm_g)
    y_sample = _mlp(xs, o_nsa_s.reshape(1, ns, D_MODEL), o_dn_s[None], merges, mod_s, g2, gf, wo, wu, wd,
                    ns, "mlp_s")
    s_win = [kw_new, vw_new]

    return (y_prompt, y_sample.reshape(ns, 1, D_MODEL),
            kv_out(kvt[0]), kv_out(kvt[1]), kv_out(kvt[2]), kv_out(kvt[3]),
            kv_out(kvt[4][:, :, seq - wlen:]), kv_out(kvt[5][:, :, seq - wlen:]),
            p_conv[None], p_dn[None],
            kvts[0][0].T.reshape(ns, 1, KV_HEADS, HEAD_DIM)[None],
            kvts[1][0].T.reshape(ns, 1, KV_HEADS, HEAD_DIM)[None],
            kvts[2][0].T.reshape(ns, 1, KV_HEADS, HEAD_DIM)[None],
            kvts[3][0].T.reshape(ns, 1, KV_HEADS, HEAD_DIM)[None],
            kv_out(s_win[0]), kv_out(s_win[1]),
            conv_new.transpose(1, 0, 2)[None], s_dn[None])
```

```python
import functools

import numpy as np
import jax
import jax.numpy as jnp
from jax import lax
from jax.experimental import pallas as pl
from jax.experimental.pallas import tpu as pltpu

F32 = jnp.float32
BF16 = jnp.bfloat16

D_MODEL = 1024
N_HEADS = 16
HEAD_DIM = 64
KV_HEADS = 4
GROUP = N_HEADS // KV_HEADS
KV_WIDTH = KV_HEADS * HEAD_DIM
CMP_STRIDE = 16
CMP_LEN = 32
CMP_HIDDEN = 128
SLC_BLOCK = 64
SLC_TOPN = 16
WINDOW = 512
DN_HEADS = 8
DN_DK = 128
DN_QKV = 3072
CONV_W = 4
D_FF = 4096
EPS = 1e-6
NEG = -1e30
LANE = 128
VMEM_LIMIT = 56 * 1024 * 1024

A_LANE = 3 * N_HEADS
B_LANE = A_LANE + DN_HEADS


def _cparams(sem):
    return pltpu.CompilerParams(dimension_semantics=sem, vmem_limit_bytes=VMEM_LIMIT)


def _dot(a, b):
    return jnp.dot(a.astype(BF16), b.astype(BF16), preferred_element_type=F32)


def _dot_nt(a, b):
    return lax.dot_general(a.astype(BF16), b.astype(BF16), (((1,), (1,)), ((), ())),
                           preferred_element_type=F32)


def _split3(x):
    hi = x.astype(BF16)
    r = x - hi.astype(F32)
    mid = r.astype(BF16)
    lo = (r - mid.astype(F32)).astype(BF16)
    return hi, mid, lo


def _dot_exact_lhs(a01, x):
    a = a01.astype(BF16)
    hi, mid, lo = _split3(x)
    return (jnp.dot(a, hi, preferred_element_type=F32) + jnp.dot(a, mid, preferred_element_type=F32)
            + jnp.dot(a, lo, preferred_element_type=F32))


def _dot_exact_rhs(x, b01):
    b = b01.astype(BF16)
    hi, mid, lo = _split3(x)
    return (jnp.dot(hi, b, preferred_element_type=F32) + jnp.dot(mid, b, preferred_element_type=F32)
            + jnp.dot(lo, b, preferred_element_type=F32))


def _sigmoid(x):
    return 1.0 / (1.0 + jnp.exp(-x))


def _softplus(x):
    return jnp.maximum(x, 0.0) + jnp.log(1.0 + jnp.exp(-jnp.abs(x)))


def _norm_mod(x, g, sc, sh):
    y = x * lax.rsqrt(jnp.mean(x * x, axis=-1, keepdims=True) + EPS)
    return (y * g) * (1.0 + sc) + sh


def _ada_kernel(c_ref, w_ref, b_ref, o_ref):
    o_ref[...] = _dot(c_ref[...], w_ref[...]) + b_ref[...]


def _ada(c_all, w_ada, b_ada):
    rows = c_all.shape[0]
    n_out = w_ada.shape[1]
    tn = D_MODEL
    return pl.pallas_call(
        _ada_kernel,
        grid=(n_out // tn,),
        in_specs=[pl.BlockSpec((rows, D_MODEL), lambda j: (0, 0)),
                  pl.BlockSpec((D_MODEL, tn), lambda j: (0, j)),
                  pl.BlockSpec((1, tn), lambda j: (0, j))],
        out_specs=pl.BlockSpec((rows, tn), lambda j: (0, j)),
        out_shape=jax.ShapeDtypeStruct((rows, n_out), F32),
        compiler_params=_cparams(("arbitrary",)),
        name="ada",
    )(c_all, w_ada, b_ada)


PROJ_CHUNK = 512


def _proj_kernel(kinds, x_ref, mod_ref, g_ref, *refs):
    nw = len(kinds)
    w_refs, o_refs = refs[:nw], refs[nw:]
    h = _norm_mod(x_ref[...], g_ref[...], mod_ref[1], mod_ref[0]).astype(BF16)
    for kind, w_ref, o_ref in zip(kinds, w_refs, o_refs):
        if kind == "nn":
            width = w_ref.shape[1]
            for c in range(0, width, PROJ_CHUNK):
                e = min(c + PROJ_CHUNK, width)
                o_ref[:, c:e] = jnp.dot(h, w_ref[:, c:e], preferred_element_type=F32).astype(o_ref.dtype)
        else:
            width = w_ref.shape[0]
            for c in range(0, width, PROJ_CHUNK):
                e = min(c + PROJ_CHUNK, width)
                o_ref[c:e, :] = lax.dot_general(w_ref[c:e, :], h, (((1,), (1,)), ((), ())),
                                                preferred_element_type=F32).astype(o_ref.dtype)


def _proj(x, mod, g, weights, kinds, dtypes, tm, name):
    G, R, _ = x.shape
    rm = mod.shape[2]
    grid = (G, R // tm)
    in_specs = [pl.BlockSpec((None, tm, D_MODEL), lambda n, i: (n, i, 0)),
                pl.BlockSpec((None, 6, rm, D_MODEL), lambda n, i: (n, 0, 0, 0)),
                pl.BlockSpec((1, D_MODEL), lambda n, i: (0, 0))]
    out_specs, out_shape = [], []
    for w, kind, dt in zip(weights, kinds, dtypes):
        in_specs.append(pl.BlockSpec(w.shape, lambda n, i: (0, 0)))
        if kind == "nn":
            width = w.shape[1]
            out_specs.append(pl.BlockSpec((None, tm, width), lambda n, i: (n, i, 0)))
            out_shape.append(jax.ShapeDtypeStruct((G, R, width), dt))
        else:
            width = w.shape[0]
            out_specs.append(pl.BlockSpec((None, width, tm), lambda n, i: (n, 0, i)))
            out_shape.append(jax.ShapeDtypeStruct((G, width, R), dt))
    return pl.pallas_call(
        functools.partial(_proj_kernel, tuple(kinds)),
        grid=grid, in_specs=in_specs, out_specs=out_specs, out_shape=out_shape,
        compiler_params=_cparams(("parallel", "parallel")),
        name=name,
    )(x, mod, g, *weights)


def _compress_core(jobs, n_tiles):
    n_chunks = n_tiles * (LANE // CMP_STRIDE)
    per_tile = LANE // CMP_STRIDE
    width = 2 * CMP_HIDDEN
    n_pairs = KV_HEADS // 2
    pe2s = []
    for _, _, _, (_, _, pe2_ref, w1f_ref, _) in jobs:
        pe_term = _dot(pe2_ref[...], w1f_ref[...])[0:1]
        pe2s.append(jnp.concatenate([pe_term, pe_term], axis=1))
    for t in range(n_tiles):
        for tile_fn, kp_ref, _, (perm_ref, _, _, _, _) in jobs:
            zs = _dot_nt(perm_ref[...], tile_fn(t))
            for pair in range(n_pairs):
                r0 = pair * n_chunks + t * per_tile
                for p in range(CMP_STRIDE):
                    kp_ref[r0:r0 + per_tile, p * LANE:(p + 1) * LANE] = (
                        zs[p * per_tile:(p + 1) * per_tile, pair * LANE:(pair + 1) * LANE])
    accs = [_dot(kp_ref[...], w[1][...]) for _, kp_ref, _, w in jobs]
    hids = []
    for pair in range(n_pairs):
        for (_, _, sh_ref, _), acc, pe2 in zip(jobs, accs, pe2s):
            a = acc[pair * n_chunks:(pair + 1) * n_chunks]
            sh_ref[pair, 0:n_chunks, :] = a[:, width:]
            sh_ref[pair, n_chunks:n_chunks + 8, :] = jnp.zeros((8, width), F32)
            hids.append(jax.nn.gelu(a[:, :width] + sh_ref[pair, 1:n_chunks + 1, :] + pe2))
    outs = [[] for _ in jobs]
    for pair in range(n_pairs):
        for gg in range(2):
            for j, (_, _, _, w) in enumerate(jobs):
                hid = hids[pair * len(jobs) + j]
                outs[j].append(_dot_nt(w[4][...], hid[:, gg * CMP_HIDDEN:(gg + 1) * CMP_HIDDEN]))
    return outs


def _compress_kernel(n_tiles, kt_ref, perm_ref, wab_ref, pe2_ref, w1f_ref, w2t_ref, o_ref, kp_ref, sh_ref):
    tile = lambda t: kt_ref[:, t * LANE:(t + 1) * LANE]
    (outs,) = _compress_core([(tile, kp_ref, sh_ref, (perm_ref, wab_ref, pe2_ref, w1f_ref, w2t_ref))], n_tiles)
    for g in range(KV_HEADS):
        o_ref[g] = outs[g]


def _compress_scratch(n_chunks):
    return [pltpu.VMEM(((KV_HEADS // 2) * n_chunks, CMP_STRIDE * LANE), F32),
            pltpu.VMEM((KV_HEADS // 2, n_chunks + 8, 2 * CMP_HIDDEN), F32)]


def _compress_prompt(kt, cw):
    n, _, length = kt.shape
    n_chunks = length // CMP_STRIDE
    const = lambda a: pl.BlockSpec(a.shape, lambda i: (0,) * a.ndim)
    return pl.pallas_call(
        functools.partial(_compress_kernel, length // LANE),
        grid=(n,),
        in_specs=[pl.BlockSpec((None, KV_WIDTH, length), lambda i: (i, 0, 0))] + [const(a) for a in cw],
        out_specs=pl.BlockSpec((None, KV_HEADS, HEAD_DIM, n_chunks), lambda i: (i, 0, 0, 0)),
        out_shape=jax.ShapeDtypeStruct((n, KV_HEADS, HEAD_DIM, n_chunks), F32),
        scratch_shapes=_compress_scratch(n_chunks),
        compiler_params=_cparams(("parallel",)),
        name="compress_prompt",
    )(kt, *cw)


def _compress_weights(pe, w1, w2):
    eye2 = jnp.eye(2, dtype=F32)
    wa = jnp.einsum("pde,ab->padbe", w1[:CMP_STRIDE], eye2).reshape(CMP_STRIDE, LANE, 2 * CMP_HIDDEN)
    wb = jnp.einsum("pde,ab->padbe", w1[CMP_STRIDE:], eye2).reshape(CMP_STRIDE, LANE, 2 * CMP_HIDDEN)
    wab = jnp.concatenate([wa, wb], axis=2).reshape(CMP_STRIDE * LANE, 4 * CMP_HIDDEN)
    pe2 = jnp.zeros((8, CMP_LEN * HEAD_DIM), F32).at[0].set(pe.reshape(-1))
    w1f = w1.reshape(CMP_LEN * HEAD_DIM, CMP_HIDDEN)
    per_tile = LANE // CMP_STRIDE
    perm = np.zeros((LANE, LANE), np.float32)
    for p in range(CMP_STRIDE):
        for i in range(per_tile):
            perm[p * per_tile + i, CMP_STRIDE * i + p] = 1.0
    return (jnp.asarray(perm, dtype=BF16), wab.astype(BF16), pe2, w1f.astype(BF16), w2.T.astype(BF16))


def _overlap_matrix(n_cmp, n_blk, rows, cols):
    c0 = np.arange(n_cmp)[:, None] * CMP_STRIDE
    b0 = np.arange(n_blk)[None, :] * SLC_BLOCK
    inter = np.minimum(c0 + CMP_LEN, b0 + SLC_BLOCK) - np.maximum(c0, b0)
    ov = np.zeros((rows, cols), np.float32)
    ov[:n_cmp, :n_blk] = np.clip(inter, 0, None) / CMP_LEN
    return jnp.asarray(ov)


def _topk_select(imp, n_blk, top_n):
    lane = lax.broadcasted_iota(jnp.int32, imp.shape, 1)
    rank = jnp.zeros(imp.shape, F32)
    for i in range(n_blk):
        col = imp[:, i:i + 1]
        ahead = (col > imp) | ((col == imp) & (lane > i))
        rank = rank + ahead.astype(F32)
    return (rank < float(top_n)).astype(F32)


def _imp_matmul(psum, ov):
    hi = psum.astype(BF16)
    lo = (psum - hi.astype(F32)).astype(BF16)
    ovb = ov.astype(BF16)
    return jnp.dot(hi, ovb, preferred_element_type=F32) + jnp.dot(lo, ovb, preferred_element_type=F32)


NSA_TQ = 256
NSA_TK_SLC = 1024
LOG2E = 1.4426950408889634


def _nsa_prompt_kernel(seq, q_ref, gate_ref, kc_ref, vc_ref, ks_ref, vs_ref, kw_ref, vw_ref, ovt_ref, eb_ref,
                       o_ref):
    tq = NSA_TQ
    g = pl.program_id(1)
    i = pl.program_id(2)
    s0 = i * tq
    n_cmp = seq // CMP_STRIDE - 1
    n_blk = seq // SLC_BLOCK
    top_n = min(SLC_TOPN, n_blk)

    q = q_ref[...].astype(F32) * (HEAD_DIM ** -0.5)
    qh = [q[:, h * HEAD_DIM:(h + 1) * HEAD_DIM].astype(BF16) for h in range(GROUP)]
    q4 = jnp.concatenate(qh, axis=0)
    qpos = s0 + lax.broadcasted_iota(jnp.int32, (tq, 1), 0)

    n_c = kc_ref.shape[1]
    s = _dot(q4, kc_ref[...]).reshape(GROUP, tq, n_c)
    cidx = lax.broadcasted_iota(jnp.int32, (tq, n_c), 1)
    valid = ((cidx * CMP_STRIDE + (CMP_LEN - 1)) <= qpos) & (cidx < n_cmp)
    sm = jnp.where(valid[None], s, NEG)
    m = jnp.max(sm, axis=-1, keepdims=True)
    p = jnp.where(valid[None], jnp.exp(sm - m), 0.0)
    p = p / jnp.maximum(jnp.sum(p, axis=-1, keepdims=True), 1e-30)
    o_cmp = _dot_nt(p.reshape(GROUP * tq, n_c), vc_ref[...])
    psum = p[0] + p[1] + p[2] + p[3]

    hi = psum.astype(BF16)
    lo = (psum - hi.astype(F32)).astype(BF16)
    imp_t = _dot_nt(ovt_ref[...], hi) + _dot_nt(ovt_ref[...], lo)
    nb8 = -(-n_blk // 8) * 8
    imp_t = imp_t[:nb8]
    blk = lax.broadcasted_iota(jnp.int32, (nb8, tq), 0)
    qrow = s0 + lax.broadcasted_iota(jnp.int32, (nb8, tq), 1)
    forced = (blk == qrow // SLC_BLOCK) | (blk == 0)
    causal = ((blk * SLC_BLOCK) <= qrow) & (blk < n_blk)
    imp_t = jnp.where(forced, jnp.inf, jnp.where(causal, imp_t, -jnp.inf))
    rank = jnp.zeros((nb8, tq), F32)
    for j in range(n_blk):
        row = imp_t[j:j + 1, :]
        ahead = (row > imp_t) | ((row == imp_t) & (blk > j))
        rank = rank + ahead.astype(F32)
    half = LANE // 2
    sel_t = jnp.where((rank < float(top_n)) & causal, 0.0, NEG)
    if nb8 < half:
        sel_t = jnp.concatenate([sel_t, jnp.full((half - nb8, tq), NEG, F32)], axis=0)
    selneg = jnp.concatenate([sel_t, sel_t], axis=0).T
    lane_q = lax.broadcasted_iota(jnp.int32, (tq, LANE), 1)
    q2 = q_ref[...].astype(F32) * (HEAD_DIM ** -0.5 * LOG2E)
    q_aug = []
    for h in range(GROUP):
        tile = q2[:, (h // 2) * LANE:(h // 2 + 1) * LANE]
        own = (lane_q < half) if h % 2 == 0 else (lane_q >= half)
        q_aug.append(jnp.where(own, tile, selneg).astype(BF16))
    tk = NSA_TK_SLC
    kt_d = (s0 + tq - 1) // tk

    def slc_tile(kt, span, carry, causal_bias):
        k0 = pl.multiple_of(kt * tk, tk)
        kT = ks_ref[:, pl.ds(k0, span)].astype(BF16)
        eT = eb_ref[:, pl.ds(k0, span)]
        k_aug = (jnp.concatenate([kT, eT], axis=0), jnp.concatenate([eT, kT], axis=0))
        vT = vs_ref[:, pl.ds(k0, span)].astype(BF16)
        if causal_bias:
            kpos = k0 + lax.broadcasted_iota(jnp.int32, (tq, span), 1)
            bias = jnp.where(kpos <= qpos, 0.0, NEG)
        out = []
        for h in range(GROUP):
            m, l, acc = carry[h]
            sm = jnp.dot(q_aug[h], k_aug[h % 2], preferred_element_type=F32)
            if causal_bias:
                sm = sm + bias
            m_new = jnp.maximum(m, jnp.max(sm, axis=-1, keepdims=True))
            alpha = jnp.exp2(m - m_new)
            p = jnp.exp2(sm - m_new)
            l = alpha * l + jnp.sum(p, axis=-1, keepdims=True)
            acc = alpha * acc + _dot_nt(p, vT)
            out.append((m_new, l, acc))
        return tuple(out)

    init = tuple((jnp.full((tq, 1), NEG, F32), jnp.zeros((tq, 1), F32), jnp.zeros((tq, HEAD_DIM), F32))
                 for _ in range(GROUP))
    n_var = tk // tq
    carry = lax.switch(i % n_var, [functools.partial(slc_tile, kt_d, (v + 1) * tq, init, True)
                                   for v in range(n_var)])
    carry = lax.fori_loop(0, kt_d, lambda j, c: slc_tile(kt_d - 1 - j, tk, c, False), carry)
    o_slc = jnp.concatenate([acc / l for _, l, acc in carry], axis=0)

    span = min(WINDOW + tq, seq)
    w0 = pl.multiple_of(jnp.minimum(jnp.maximum(s0 - WINDOW, 0), seq - span), tq)
    kT = kw_ref[:, pl.ds(w0, span)].astype(BF16)
    vT = vw_ref[:, pl.ds(w0, span)].astype(BF16)
    kpos = w0 + lax.broadcasted_iota(jnp.int32, (tq, span), 1)
    bias = jnp.where((kpos <= qpos) & (kpos > qpos - WINDOW), 0.0, NEG)
    o_win = []
    for h in range(GROUP):
        sm = _dot(qh[h], kT) + bias
        p = jnp.exp(sm - jnp.max(sm, axis=-1, keepdims=True))
        o_win.append(_dot_nt(p, vT) / jnp.sum(p, axis=-1, keepdims=True))
    o_win = jnp.concatenate(o_win, axis=0)

    sg = _sigmoid(gate_ref[...])
    lane = lax.broadcasted_iota(jnp.int32, (tq, LANE), 1)
    for h in range(GROUP):
        base = g * (GROUP * 3) + h * 3
        gc = [jnp.sum(jnp.where(lane == base + b, sg, 0.0), axis=-1, keepdims=True) for b in range(3)]
        rows = slice(h * tq, (h + 1) * tq)
        out = gc[0] * o_cmp[rows] + gc[1] * o_slc[rows] + gc[2] * o_win[rows]
        o_ref[:, h * HEAD_DIM:(h + 1) * HEAD_DIM] = out


def _nsa_prompt(q, small, kct, vct, kst, vst, kwt, vwt):
    n, seq, _ = q.shape
    n_c = kct.shape[-1]
    assert NSA_TK_SLC % NSA_TQ == 0 and seq % NSA_TK_SLC == 0 and WINDOW % NSA_TQ == 0
    assert seq // SLC_BLOCK <= LANE // 2
    ov = _overlap_matrix(seq // CMP_STRIDE - 1, seq // SLC_BLOCK, n_c, LANE).T
    eb = np.zeros((LANE // 2, seq), np.float32)
    eb[np.arange(seq) // SLC_BLOCK, np.arange(seq)] = 1.0
    eb = jnp.asarray(eb, dtype=BF16)
    cw = GROUP * HEAD_DIM
    head_spec = lambda width: pl.BlockSpec((None, None, HEAD_DIM, width), lambda b, g, i: (b, g, 0, 0))
    r4 = lambda a: a.reshape(n, KV_HEADS, HEAD_DIM, seq)
    return pl.pallas_call(
        functools.partial(_nsa_prompt_kernel, seq),
        grid=(n, KV_HEADS, seq // NSA_TQ),
        in_specs=[pl.BlockSpec((None, NSA_TQ, cw), lambda b, g, i: (b, i, g)),
                  pl.BlockSpec((None, NSA_TQ, LANE), lambda b, g, i: (b, i, 0)),
                  head_spec(n_c), head_spec(n_c), head_spec(seq), head_spec(seq), head_spec(seq), head_spec(seq),
                  pl.BlockSpec(ov.shape, lambda b, g, i: (0, 0)),
                  pl.BlockSpec(eb.shape, lambda b, g, i: (0, 0))],
        out_specs=pl.BlockSpec((None, NSA_TQ, cw), lambda b, g, i: (b, i, g)),
        out_shape=jax.ShapeDtypeStruct((n, seq, D_MODEL), F32),
        compiler_params=_cparams(("parallel", "parallel", "parallel")),
        name="nsa_prompt",
    )(q, small, kct, vct, r4(kst), r4(vst), r4(kwt), r4(vwt), ov, eb)


DN_C = 128


def _dn_gates(small, alog_ref, dt_ref):
    g = -jnp.exp(alog_ref[...]) * _softplus(small + dt_ref[...])
    return g, _sigmoid(small)


def _l2n(x):
    return x * lax.rsqrt(jnp.sum(x * x, axis=-1, keepdims=True) + EPS)


def _dn_out(o, z, ng):
    on = o * lax.rsqrt(jnp.mean(o * o, axis=-1, keepdims=True) + EPS) * ng
    return on * (z * _sigmoid(z))


INV_BASE = 16


def _unit_lower_inverse(lmats, eye, ri, ci, c):
    same = (ri // INV_BASE) == (ci // INV_BASE)
    pws = [jnp.where(same, -l, 0.0) for l in lmats]
    xs = [eye + pw for pw in pws]
    pws = [_dot(pw, pw) for pw in pws]
    steps = int(np.log2(INV_BASE)) - 1
    for k in range(steps):
        if k < steps - 1:
            rs = [_dot(pw, jnp.concatenate([pw, x], axis=1)) for pw, x in zip(pws, xs)]
            xs = [x + r[:, c:] for x, r in zip(xs, rs)]
            pws = [r[:, :c] for r in rs]
        else:
            xs = [x + _dot(pw, x) for x, pw in zip(xs, pws)]
    s = INV_BASE
    while s < c:
        off = ((ri // (2 * s)) == (ci // (2 * s))) & ((ri // s) != (ci // s))
        ts = [_dot(x, jnp.where(off, l, 0.0)) for x, l in zip(xs, lmats)]
        xs = [x - _dot(t, x) for x, t in zip(xs, ts)]
        s *= 2
    return xs


def _dn_prompt_kernel(qkv_ref, z_ref, small_ref, cw_ref, alog_ref, dt_ref, ng_ref, o_ref, st_ref, xbuf, s_ref):
    c = DN_C
    i = pl.program_id(1)
    last = pl.num_programs(1) - 1

    @pl.when(i == 0)
    def _():
        xbuf[0:8, :] = jnp.zeros((8, DN_QKV), F32)
        s_ref[...] = jnp.zeros_like(s_ref)

    xbuf[8:8 + c, :] = qkv_ref[...]
    y = (cw_ref[3:4, :] * xbuf[8:8 + c, :] + cw_ref[2:3, :] * xbuf[7:7 + c, :]
         + cw_ref[1:2, :] * xbuf[6:6 + c, :] + cw_ref[0:1, :] * xbuf[5:5 + c, :])
    xbuf[0:8, :] = xbuf[c:c + 8, :]
    y = y * _sigmoid(y)

    gl, bt = _dn_gates(small_ref[...], alog_ref, dt_ref)
    ri = lax.broadcasted_iota(jnp.int32, (c, c), 0)
    ci = lax.broadcasted_iota(jnp.int32, (c, c), 1)
    incl = ri >= ci
    strict = ri > ci
    gc_col = _dot_exact_lhs(incl.astype(F32), gl)
    gc_row = _dot_exact_rhs(gl.T, (ri <= ci).astype(F32))
    eye = (ri == ci).astype(F32)
    nd = DN_HEADS * DN_DK

    hs = range(DN_HEADS)
    sls = [slice(h * DN_DK, (h + 1) * DN_DK) for h in hs]
    q = [_l2n(y[:, sl]) * (DN_DK ** -0.5) for sl in sls]
    k = [_l2n(y[:, nd + h * DN_DK:nd + (h + 1) * DN_DK]) for h in hs]
    v = [y[:, 2 * nd + h * DN_DK:2 * nd + (h + 1) * DN_DK] for h in hs]
    gcc = [gc_col[:, A_LANE + h:A_LANE + h + 1] for h in hs]
    gcr = [gc_row[A_LANE + h:A_LANE + h + 1, :] for h in hs]
    beta = [bt[:, B_LANE + h:B_LANE + h + 1] for h in hs]
    decay = [jnp.exp(jnp.where(incl, gcc[h] - gcr[h], NEG)) for h in hs]
    kb = [k[h] * beta[h] for h in hs]
    kq = [_dot_nt(jnp.concatenate([kb[h], q[h]], axis=0), k[h]) for h in hs]
    lmat = [jnp.where(strict, kq[h][:c] * decay[h], 0.0) for h in hs]
    a_intra = [kq[h][c:] * decay[h] for h in hs]
    tmat = _unit_lower_inverse(lmat, eye, ri, ci, c)
    eg = [jnp.exp(gcc[h]) for h in hs]
    uw = [_dot(tmat[h], jnp.concatenate([v[h] * beta[h], kb[h] * eg[h]], axis=1)) for h in hs]
    g_last = [gcr[h][:, c - 1:c] for h in hs]
    k_dec_t = [(k[h] * jnp.exp(g_last[h] - gcc[h])).T for h in hs]
    state = [s_ref[h] for h in hs]
    ws = [_dot(jnp.concatenate([uw[h][:, DN_DK:], q[h] * eg[h]], axis=0), state[h]) for h in hs]
    v_new = [uw[h][:, :DN_DK] - ws[h][:c] for h in hs]
    o = [ws[h][c:] + _dot(a_intra[h], v_new[h]) for h in hs]
    for h in hs:
        s_ref[h] = state[h] * jnp.exp(g_last[h]) + _dot(k_dec_t[h], v_new[h])
    for h in hs:
        o_ref[:, sls[h]] = _dn_out(o[h], z_ref[:, sls[h]], ng_ref[...])

    @pl.when(i == last)
    def _():
        st_ref[...] = s_ref[...]


def _dn_prompt(qkv, z, small, conv_w, alog_row, dt_row, norm_g):
    n, seq, _ = qkv.shape
    c = DN_C
    const = lambda a: pl.BlockSpec(a.shape, lambda b, i: (0,) * a.ndim)
    return pl.pallas_call(
        _dn_prompt_kernel,
        grid=(n, seq // c),
        in_specs=[pl.BlockSpec((None, c, DN_QKV), lambda b, i: (b, i, 0)),
                  pl.BlockSpec((None, c, D_MODEL), lambda b, i: (b, i, 0)),
                  pl.BlockSpec((None, c, LANE), lambda b, i: (b, i, 0)),
                  const(conv_w), const(alog_row), const(dt_row), const(norm_g)],
        out_specs=[pl.BlockSpec((None, c, D_MODEL), lambda b, i: (b, i, 0)),
                   pl.BlockSpec((None, DN_HEADS, DN_DK, DN_DK), lambda b, i: (b, 0, 0, 0))],
        out_shape=[jax.ShapeDtypeStruct((n, seq, D_MODEL), F32),
                   jax.ShapeDtypeStruct((n, DN_HEADS, DN_DK, DN_DK), F32)],
        scratch_shapes=[pltpu.VMEM((c + 8, DN_QKV), F32), pltpu.VMEM((DN_HEADS, DN_DK, DN_DK), F32)],
        compiler_params=_cparams(("parallel", "arbitrary")),
        name="dn_prompt",
    )(qkv, z, small, conv_w, alog_row, dt_row, norm_g)


MLP_TF = 2048


def _mlp_kernel(x_ref, on_ref, od_ref, mg_ref, mod_ref, g2_ref, gf_ref, wo_ref, wu_ref, wd_ref, y_ref,
                x1_ref, h2_ref, acc_ref):
    j = pl.program_id(2)
    last = pl.num_programs(2) - 1

    @pl.when(j == 0)
    def _():
        ga = _sigmoid(mg_ref[:, :D_MODEL])
        gb = _sigmoid(mg_ref[:, D_MODEL:])
        mixed = ga * on_ref[...] + gb * od_ref[...]
        x1 = x_ref[...] + mod_ref[2] * _dot(mixed, wo_ref[...])
        x1_ref[...] = x1
        h2_ref[...] = _norm_mod(x1, g2_ref[...], mod_ref[4], mod_ref[3]).astype(BF16)
        acc_ref[...] = jnp.zeros_like(acc_ref)

    up = jnp.dot(h2_ref[...], wu_ref[...], preferred_element_type=F32)
    act = jnp.square(jnp.maximum(up, 0.0))
    acc_ref[...] += _dot(act, wd_ref[...])

    @pl.when(j == last)
    def _():
        x2 = x1_ref[...] + mod_ref[5] * acc_ref[...]
        y_ref[...] = x2 * lax.rsqrt(jnp.mean(x2 * x2, axis=-1, keepdims=True) + EPS) * gf_ref[...]


def _mlp(x, o_nsa, o_dn, merge, mod, g2, gf, w_out, w_up, w_down, tm, name):
    G, R, _ = x.shape
    rm = mod.shape[2]
    tf = MLP_TF
    row = lambda width: pl.BlockSpec((None, tm, width), lambda n, i, j: (n, i, 0))
    return pl.pallas_call(
        _mlp_kernel,
        grid=(G, R // tm, D_FF // tf),
        in_specs=[row(D_MODEL), row(D_MODEL), row(D_MODEL), row(2 * D_MODEL),
                  pl.BlockSpec((None, 6, rm, D_MODEL), lambda n, i, j: (n, 0, 0, 0)),
                  pl.BlockSpec((1, D_MODEL), lambda n, i, j: (0, 0)),
                  pl.BlockSpec((1, D_MODEL), lambda n, i, j: (0, 0)),
                  pl.BlockSpec((D_MODEL, D_MODEL), lambda n, i, j: (0, 0)),
                  pl.BlockSpec((D_MODEL, tf), lambda n, i, j: (0, j)),
                  pl.BlockSpec((tf, D_MODEL), lambda n, i, j: (j, 0))],
        out_specs=row(D_MODEL),
        out_shape=jax.ShapeDtypeStruct((G, R, D_MODEL), F32),
        scratch_shapes=[pltpu.VMEM((tm, D_MODEL), F32), pltpu.VMEM((tm, D_MODEL), BF16),
                        pltpu.VMEM((tm, D_MODEL), F32)],
        compiler_params=_cparams(("parallel", "parallel", "arbitrary")),
        name=name,
    )(x, o_nsa, o_dn, merge, mod, g2, gf, w_out, w_up, w_down)


def _nsa_sample_kernel(n_pages, page, win, *refs):
    n_in = 4 * n_pages
    pt_ref = refs[0]
    del pt_ref
    pools = [refs[1 + t * n_pages:1 + (t + 1) * n_pages] for t in range(4)]
    (q_ref, gate_ref, newt_ref, kwc_ref, vwc_ref, ov_ref, ex_ref) = refs[1 + n_in:1 + n_in + 7]
    rows16 = newt_ref[...]
    new_t = jnp.concatenate([rows16, jnp.zeros((LANE - rows16.shape[0], LANE), F32)], axis=0).T
    new_ref = [jnp.concatenate([new_t[:, 2 * t:2 * t + 1], new_t[:, 2 * t + 1:2 * t + 2]], axis=0)
               for t in range(6)]
    cwk = refs[1 + n_in + 7:1 + n_in + 12]
    cwv = refs[1 + n_in + 12:1 + n_in + 17]
    o_ref, kwo_ref, vwo_ref = refs[1 + n_in + 17:1 + n_in + 20]
    full_a, full_b, kpk_ref, shk_ref, kpv_ref, shv_ref = refs[1 + n_in + 20:]

    past = n_pages * page
    total = past + 1
    lp = -(-total // SLC_BLOCK) * SLC_BLOCK
    n_cmp = lp // CMP_STRIDE - 1
    n_blk = lp // SLC_BLOCK
    top_n = min(SLC_TOPN, n_blk)
    length = full_a.shape[1]
    n_c = length // CMP_STRIDE
    n_tiles = length // LANE

    def fill(full_ref, page_refs, t):
        for j in range(n_pages):
            full_ref[:, j * page:(j + 1) * page] = page_refs[j][...]
        full_ref[:, past:length] = jnp.zeros((KV_WIDTH, length - past), F32)
        full_ref[:, past:past + 1] = new_ref[t]

    def cmp_tiles(page_refs, t):
        def tile(j):
            if j < n_pages:
                return page_refs[j][...]
            if j == n_pages:
                lane = lax.broadcasted_iota(jnp.int32, (KV_WIDTH, LANE), 1)
                return jnp.where(lane == 0, new_ref[t], 0.0)
            return jnp.zeros((KV_WIDTH, LANE), F32)
        return tile

    q16 = (q_ref[...].astype(F32) * (HEAD_DIM ** -0.5)).astype(BF16)
    head_group = lax.broadcasted_iota(jnp.int32, (N_HEADS, 1), 0) // GROUP

    def by_group(vals):
        out = vals[KV_HEADS - 1]
        for g in range(KV_HEADS - 2, -1, -1):
            out = jnp.where(head_group == g, vals[g], out)
        return out

    def attend(kt, vt, valid):
        rows = [slice(g * HEAD_DIM, (g + 1) * HEAD_DIM) for g in range(KV_HEADS)]
        s = by_group([_dot(q16, kt[r]) for r in rows])
        sm = jnp.where(valid, s, NEG)
        m = jnp.max(sm, axis=-1, keepdims=True)
        p = jnp.where(valid, jnp.exp(sm - m), 0.0)
        p = p / jnp.maximum(jnp.sum(p, axis=-1, keepdims=True), 1e-30)
        o = by_group([_dot_nt(p, vt[r]) for r in rows])
        return p, o

    lane_w = lax.broadcasted_iota(jnp.int32, (KV_WIDTH, win), 1)
    kw = jnp.where(lane_w == win - 1, new_ref[4], pltpu.roll(kwc_ref[...], win - 1, axis=1))
    vw = jnp.where(lane_w == win - 1, new_ref[5], pltpu.roll(vwc_ref[...], win - 1, axis=1))
    kwo_ref[...] = kw
    vwo_ref[...] = vw
    wpos = past - win + 1 + lax.broadcasted_iota(jnp.int32, (1, win), 1)
    valid_w = (wpos <= past) & (wpos > past - WINDOW) & (wpos >= 0)
    _, o_win = attend(kw, vw, valid_w)
    fill(full_a, pools[2], 2)
    fill(full_b, pools[3], 3)

    kcs, vcs = _compress_core([(cmp_tiles(pools[0], 0), kpk_ref, shk_ref, cwk),
                               (cmp_tiles(pools[1], 1), kpv_ref, shv_ref, cwv)], n_tiles)
    cidx = lax.broadcasted_iota(jnp.int32, (1, n_c), 1)
    valid_c = ((cidx * CMP_STRIDE + (CMP_LEN - 1)) <= past) & (cidx < n_cmp)
    p_cmp, o_cmp = attend(jnp.concatenate(kcs, axis=0), jnp.concatenate(vcs, axis=0), valid_c)
    imp_h = _imp_matmul(p_cmp, ov_ref[...])
    hr = lax.broadcasted_iota(jnp.int32, (N_HEADS, N_HEADS), 0) // GROUP
    hc = lax.broadcasted_iota(jnp.int32, (N_HEADS, N_HEADS), 1) // GROUP
    imp = _dot_exact_lhs((hr == hc).astype(F32), imp_h)
    blk = lax.broadcasted_iota(jnp.int32, (N_HEADS, LANE), 1)
    forced = (blk == past // SLC_BLOCK) | (blk == 0)
    causal = (blk * SLC_BLOCK) <= past
    imp = jnp.where(forced, jnp.inf, jnp.where(causal & (blk < n_blk), imp, -jnp.inf))
    sel = _topk_select(imp, n_blk, top_n).astype(BF16)

    kpos = lax.broadcasted_iota(jnp.int32, (1, length), 1)
    chosen = jnp.dot(sel, ex_ref[...], preferred_element_type=F32) > 0.5
    valid_s = chosen & (kpos <= past)
    _, o_slc = attend(full_a[...], full_b[...], valid_s)

    sg = _sigmoid(gate_ref[...])
    o_ref[...] = sg[:, 0:1] * o_cmp + sg[:, 1:2] * o_slc + sg[:, 2:3] * o_win


def _nsa_sample(page_table, pools, q16, gates, newcols, kwc, vwc, cwk, cwv):
    n_seq, n_pages = page_table.shape
    page = pools[0].shape[-1]
    win = kwc.shape[-1]
    past = n_pages * page
    lp = -(-(past + 1) // SLC_BLOCK) * SLC_BLOCK
    length = -(-lp // LANE) * LANE
    n_c = length // CMP_STRIDE
    ov = _overlap_matrix(lp // CMP_STRIDE - 1, lp // SLC_BLOCK, n_c, LANE)
    ex = np.zeros((LANE, length), np.float32)
    ex[np.arange(length) // SLC_BLOCK, np.arange(length)] = 1.0
    ex = jnp.asarray(ex, dtype=BF16)
    assert win == WINDOW and page == LANE

    in_specs = []
    for _ in range(4):
        for j in range(n_pages):
            in_specs.append(pl.BlockSpec((None, KV_WIDTH, page), functools.partial(
                lambda s, pt, jj: (pt[s, jj], 0, 0), jj=j)))
    const = lambda a: pl.BlockSpec(a.shape, lambda s, pt: (0,) * a.ndim)
    in_specs += [pl.BlockSpec((None, N_HEADS, HEAD_DIM), lambda s, pt: (s, 0, 0)),
                 pl.BlockSpec((None, N_HEADS, 3), lambda s, pt: (s, 0, 0)),
                 pl.BlockSpec((None,) + newcols.shape[1:], lambda s, pt: (s, 0, 0)),
                 pl.BlockSpec((None, KV_WIDTH, win), lambda s, pt: (s, 0, 0)),
                 pl.BlockSpec((None, KV_WIDTH, win), lambda s, pt: (s, 0, 0)),
                 const(ov), const(ex)] + [const(a) for a in cwk] + [const(a) for a in cwv]
    operands = []
    for t in range(4):
        operands += [pools[t]] * n_pages
    operands += [q16, gates, newcols, kwc, vwc, ov, ex, *cwk, *cwv]
    win_spec = pl.BlockSpec((None, KV_WIDTH, win), lambda s, pt: (s, 0, 0))
    grid_spec = pltpu.PrefetchScalarGridSpec(
        num_scalar_prefetch=1, grid=(n_seq,), in_specs=in_specs,
        out_specs=[pl.BlockSpec((None, N_HEADS, HEAD_DIM), lambda s, pt: (s, 0, 0)), win_spec, win_spec],
        scratch_shapes=[pltpu.VMEM((KV_WIDTH, length), F32), pltpu.VMEM((KV_WIDTH, length), F32)]
        + _compress_scratch(n_c) + _compress_scratch(n_c))
    return pl.pallas_call(
        functools.partial(_nsa_sample_kernel, n_pages, page, win),
        grid_spec=grid_spec,
        out_shape=[jax.ShapeDtypeStruct((n_seq, N_HEADS, HEAD_DIM), F32),
                   jax.ShapeDtypeStruct((n_seq, KV_WIDTH, win), F32),
                   jax.ShapeDtypeStruct((n_seq, KV_WIDTH, win), F32)],
        compiler_params=_cparams(("arbitrary",)),
        name="nsa_sample",
    )(page_table, *operands)


DN_SB = 8


def _dn_sample_kernel(qkv_ref, conv_ref, z_ref, small_ref, st_ref, cw_ref, alog_ref, dt_ref, ng_ref,
                      o_ref, conv_o_ref, st_o_ref):
    x = qkv_ref[...]
    y = (cw_ref[0:1, :] * conv_ref[0] + cw_ref[1:2, :] * conv_ref[1] + cw_ref[2:3, :] * conv_ref[2]
         + cw_ref[3:4, :] * x)
    conv_o_ref[0] = conv_ref[1]
    conv_o_ref[1] = conv_ref[2]
    conv_o_ref[2] = x
    y = y * _sigmoid(y)
    gl, bt = _dn_gates(small_ref[...], alog_ref, dt_ref)
    nd = DN_HEADS * DN_DK
    ri = lax.broadcasted_iota(jnp.int32, (DN_DK, DN_DK), 0)
    ci = lax.broadcasted_iota(jnp.int32, (DN_DK, DN_DK), 1)
    eye = ri == ci
    for h in range(DN_HEADS):
        sl = slice(h * DN_DK, (h + 1) * DN_DK)
        q = _l2n(y[:, sl]) * (DN_DK ** -0.5)
        k = _l2n(y[:, nd + h * DN_DK:nd + (h + 1) * DN_DK])
        v = y[:, 2 * nd + h * DN_DK:2 * nd + (h + 1) * DN_DK]
        g = gl[:, A_LANE + h:A_LANE + h + 1]
        beta = bt[:, B_LANE + h:B_LANE + h + 1]
        eg = jnp.exp(g)
        u = v * beta
        w = k * beta * eg
        a_intra = jnp.sum(q * k, axis=-1, keepdims=True)
        q_dec = q * eg
        z = z_ref[:, sl]
        for b in range(DN_SB):
            state = st_ref[b, h]
            lhs = jnp.concatenate([w[b:b + 1], q_dec[b:b + 1], jnp.zeros((6, DN_DK), F32)], axis=0)
            res = _dot(lhs, state)
            v_new = u[b:b + 1] - res[0:1]
            o = res[1:2] + a_intra[b:b + 1] * v_new
            k_col = jnp.sum(jnp.where(eye, k[b:b + 1], 0.0), axis=-1, keepdims=True)
            st_o_ref[b, h] = state * eg[b:b + 1] + k_col * v_new
            o_ref[b:b + 1, sl] = _dn_out(o, z[b:b + 1], ng_ref[...])


def _dn_sample(qkv, conv_t, z, small, state, conv_w, alog_row, dt_row, norm_g):
    n_seq = qkv.shape[0]
    sb = DN_SB
    const = lambda a: pl.BlockSpec(a.shape, lambda i: (0,) * a.ndim)
    return pl.pallas_call(
        _dn_sample_kernel,
        grid=(n_seq // sb,),
        in_specs=[pl.BlockSpec((sb, DN_QKV), lambda i: (i, 0)),
                  pl.BlockSpec((CONV_W - 1, sb, DN_QKV), lambda i: (0, i, 0)),
                  pl.BlockSpec((sb, D_MODEL), lambda i: (i, 0)),
                  pl.BlockSpec((sb, LANE), lambda i: (i, 0)),
                  pl.BlockSpec((sb, DN_HEADS, DN_DK, DN_DK), lambda i: (i, 0, 0, 0)),
                  const(conv_w), const(alog_row), const(dt_row), const(norm_g)],
        out_specs=[pl.BlockSpec((sb, D_MODEL), lambda i: (i, 0)),
                   pl.BlockSpec((CONV_W - 1, sb, DN_QKV), lambda i: (0, i, 0)),
                   pl.BlockSpec((sb, DN_HEADS, DN_DK, DN_DK), lambda i: (i, 0, 0, 0))],
        out_shape=[jax.ShapeDtypeStruct((n_seq, D_MODEL), F32),
                   jax.ShapeDtypeStruct((CONV_W - 1, n_seq, DN_QKV), F32),
                   jax.ShapeDtypeStruct(state.shape, F32)],
        compiler_params=_cparams(("parallel",)),
        name="dn_sample",
    )(qkv, conv_t, z, small, state, conv_w, alog_row, dt_row, norm_g)


def _pick_tile(rows, pref):
    t = min(rows, pref)
    while rows % t:
        t //= 2
    return t


def _layer_weights(w_in):
    w = w_in.astype(BF16)
    o = 0
    wq = w[:, o:o + D_MODEL]; o += D_MODEL
    wkv = [w[:, o + t * KV_WIDTH:o + (t + 1) * KV_WIDTH].T for t in range(6)]; o += 6 * KV_WIDTH
    wg = w[:, o:o + 3 * N_HEADS]; o += 3 * N_HEADS
    wqkv = w[:, o:o + DN_QKV]; o += DN_QKV
    wz = w[:, o:o + D_MODEL]; o += D_MODEL
    wa = w[:, o:o + DN_HEADS]; o += DN_HEADS
    wb = w[:, o:o + DN_HEADS]; o += DN_HEADS
    wm = w[:, o:o + 2 * D_MODEL]
    wsmall = jnp.concatenate([wg, wa, wb, jnp.zeros((D_MODEL, LANE - B_LANE - DN_HEADS), BF16)], axis=1)
    return wq, wkv, wqkv, wz, wm, wsmall


def kernel(x_prompt, x_sample, c_prompt, c_sample, cache_k_cmp, cache_v_cmp, cache_k_slc, cache_v_slc, cache_k_win, cache_v_win, state_conv, state_dn, page_table, w_ada, b_ada, norm1_g, norm2_g, w_in, cmp_pe_k, cmp_w1_k, cmp_w2_k, cmp_pe_v, cmp_w1_v, cmp_w2_v, dn_conv_w, dn_a_log, dn_dt_bias, dn_norm_g, w_out, w_up, w_down, final_g):
    assert w_ada.shape[0] == 1, "single-layer stack"
    nb, seq, _ = x_prompt.shape
    ns = x_sample.shape[0]
    assert x_sample.shape[1] == 1

    n_c = nb + ns
    rows = -(-n_c // 8) * 8
    c_all = jnp.concatenate([c_prompt, c_sample, jnp.zeros((rows - n_c, D_MODEL), F32)], axis=0)
    mod = _ada(c_all, w_ada[0], b_ada)
    mod_p = mod[:nb].reshape(nb, 6, 1, D_MODEL)
    mod_s = mod[nb:n_c].reshape(ns, 6, D_MODEL).transpose(1, 0, 2)[None]

    wq, wkv, wqkv, wz, wm, wsmall = _layer_weights(w_in[0])
    g1 = norm1_g
    g2 = norm2_g
    gf = final_g.reshape(1, D_MODEL)
    alog_row = jnp.zeros((1, LANE), F32).at[0, A_LANE:A_LANE + DN_HEADS].set(dn_a_log[0])
    dt_row = jnp.zeros((1, LANE), F32).at[0, A_LANE:A_LANE + DN_HEADS].set(dn_dt_bias[0])
    conv_w = dn_conv_w[0]
    cwk = _compress_weights(cmp_pe_k[0], cmp_w1_k[0], cmp_w2_k[0])
    cwv = _compress_weights(cmp_pe_v[0], cmp_w1_v[0], cmp_w2_v[0])
    wo = w_out[0].astype(BF16)
    wu = w_up[0].astype(BF16)
    wd = w_down[0].astype(BF16)

    def project(x, modg, tm, tag):
        q, *kvt = _proj(x, modg, g1, [wq] + wkv, ["nn"] + ["nt"] * 6, [BF16] + [F32] * 6, tm, "proj_a_" + tag)
        (qkv,) = _proj(x, modg, g1, [wqkv], ["nn"], [F32], tm, "proj_b_" + tag)
        z, merge, small = _proj(x, modg, g1, [wz, wm, wsmall], ["nn"] * 3, [F32] * 3, tm, "proj_c_" + tag)
        return q, kvt, qkv, z, merge, small

    def kv_out(a):
        n, _, length = a.shape
        return a.reshape(n, KV_HEADS, HEAD_DIM, length).transpose(0, 3, 1, 2)[None]

    tm_p = _pick_tile(seq, 512)
    q, kvt, qkv, z, merge, small = project(x_prompt, mod_p, tm_p, "p")
    kct = _compress_prompt(kvt[0], cwk)
    vct = _compress_prompt(kvt[1], cwv)
    o_nsa = _nsa_prompt(q, small, kct, vct, kvt[2], kvt[3], kvt[4], kvt[5])
    o_dn, p_dn = _dn_prompt(qkv, z, small, conv_w, alog_row, dt_row, dn_norm_g)
    y_prompt = _mlp(x_prompt, o_nsa, o_dn, merge, mod_p, g2, gf, wo, wu, wd, tm_p, "mlp_p")
    wlen = min(WINDOW, seq)
    p_conv = qkv[:, seq - (CONV_W - 1):, :]

    xs = x_sample.reshape(1, ns, D_MODEL)
    qs, kvts, qkvs, zs, merges, smalls = project(xs, mod_s, ns, "s")
    pools = [c[0].transpose(0, 2, 3, 1).reshape(c.shape[1], KV_WIDTH, c.shape[2])
             for c in (cache_k_cmp, cache_v_cmp, cache_k_slc, cache_v_slc)]
    wins = [c[0].transpose(0, 2, 3, 1).reshape(ns, KV_WIDTH, c.shape[2]) for c in (cache_k_win, cache_v_win)]
    newrows = jnp.stack([a[0].T.reshape(ns, KV_WIDTH // LANE, LANE) for a in kvts], axis=1)
    newcols = jnp.pad(newrows.reshape(ns, 6 * (KV_WIDTH // LANE), LANE), ((0, 0), (0, 4), (0, 0)))
    q16 = qs.reshape(ns, N_HEADS, HEAD_DIM)
    gates = smalls[0, :, :3 * N_HEADS].reshape(ns, N_HEADS, 3)
    o_nsa_s, kw_new, vw_new = _nsa_sample(page_table, pools, q16, gates, newcols, wins[0], wins[1], cwk, cwv)
    conv_t = state_conv[0].transpose(1, 0, 2)
    o_dn_s, conv_new, s_dn = _dn_sample(qkvs[0], conv_t, zs[0], smalls[0], state_dn[0], conv_w, alog_row,
                                        dt_row, dn_norm_g)
    y_sample = _mlp(xs, o_nsa_s.reshape(1, ns, D_MODEL), o_dn_s[None], merges, mod_s, g2, gf, wo, wu, wd,
                    ns, "mlp_s")
    s_win = [kw_new, vw_new]

    return (y_prompt, y_sample.reshape(ns, 1, D_MODEL),
            kv_out(kvt[0]), kv_out(kvt[1]), kv_out(kvt[2]), kv_out(kvt[3]),
            kv_out(kvt[4][:, :, seq - wlen:]), kv_out(kvt[5][:, :, seq - wlen:]),
            p_conv[None], p_dn[None],
            kvts[0][0].T.reshape(ns, 1, KV_HEADS, HEAD_DIM)[None],
            kvts[1][0].T.reshape(ns, 1, KV_HEADS, HEAD_DIM)[None],
            kvts[2][0].T.reshape(ns, 1, KV_HEADS, HEAD_DIM)[None],
            kvts[3][0].T.reshape(ns, 1, KV_HEADS, HEAD_DIM)[None],
            kv_out(s_win[0]), kv_out(s_win[1]),
            conv_new.transpose(1, 0, 2)[None], s_dn[None])
```

```python
import functools

import numpy as np
import jax
import jax.numpy as jnp
from jax import lax
from jax.experimental import pallas as pl
from jax.experimental.pallas import tpu as pltpu

F32 = jnp.float32
BF16 = jnp.bfloat16

D_MODEL = 1024
N_HEADS = 16
HEAD_DIM = 64
KV_HEADS = 4
GROUP = N_HEADS // KV_HEADS
KV_WIDTH = KV_HEADS * HEAD_DIM
CMP_STRIDE = 16
CMP_LEN = 32
CMP_HIDDEN = 128
SLC_BLOCK = 64
SLC_TOPN = 16
WINDOW = 512
DN_HEADS = 8
DN_DK = 128
DN_QKV = 3072
CONV_W = 4
D_FF = 4096
EPS = 1e-6
NEG = -1e30
LANE = 128
VMEM_LIMIT = 56 * 1024 * 1024

A_LANE = 3 * N_HEADS
B_LANE = A_LANE + DN_HEADS


def _cparams(sem):
    return pltpu.CompilerParams(dimension_semantics=sem, vmem_limit_bytes=VMEM_LIMIT)


def _dot(a, b):
    return jnp.dot(a.astype(BF16), b.astype(BF16), preferred_element_type=F32)


def _dot_nt(a, b):
    return lax.dot_general(a.astype(BF16), b.astype(BF16), (((1,), (1,)), ((), ())),
                           preferred_element_type=F32)


def _split3(x):
    hi = x.astype(BF16)
    r = x - hi.astype(F32)
    mid = r.astype(BF16)
    lo = (r - mid.astype(F32)).astype(BF16)
    return hi, mid, lo


def _dot_exact_lhs(a01, x):
    a = a01.astype(BF16)
    hi, mid, lo = _split3(x)
    return (jnp.dot(a, hi, preferred_element_type=F32) + jnp.dot(a, mid, preferred_element_type=F32)
            + jnp.dot(a, lo, preferred_element_type=F32))


def _dot_exact_rhs(x, b01):
    b = b01.astype(BF16)
    hi, mid, lo = _split3(x)
    return (jnp.dot(hi, b, preferred_element_type=F32) + jnp.dot(mid, b, preferred_element_type=F32)
            + jnp.dot(lo, b, preferred_element_type=F32))


def _sigmoid(x):
    return 1.0 / (1.0 + jnp.exp(-x))


def _softplus(x):
    return jnp.maximum(x, 0.0) + jnp.log(1.0 + jnp.exp(-jnp.abs(x)))


def _norm_mod(x, g, sc, sh):
    y = x * lax.rsqrt(jnp.mean(x * x, axis=-1, keepdims=True) + EPS)
    return (y * g) * (1.0 + sc) + sh


def _ada_kernel(c_ref, w_ref, b_ref, o_ref):
    o_ref[...] = _dot(c_ref[...], w_ref[...]) + b_ref[...]


def _ada(c_all, w_ada, b_ada):
    rows = c_all.shape[0]
    n_out = w_ada.shape[1]
    tn = D_MODEL
    return pl.pallas_call(
        _ada_kernel,
        grid=(n_out // tn,),
        in_specs=[pl.BlockSpec((rows, D_MODEL), lambda j: (0, 0)),
                  pl.BlockSpec((D_MODEL, tn), lambda j: (0, j)),
                  pl.BlockSpec((1, tn), lambda j: (0, j))],
        out_specs=pl.BlockSpec((rows, tn), lambda j: (0, j)),
        out_shape=jax.ShapeDtypeStruct((rows, n_out), F32),
        compiler_params=_cparams(("arbitrary",)),
        name="ada",
    )(c_all, w_ada, b_ada)


PROJ_CHUNK = 512


def _proj_kernel(kinds, x_ref, mod_ref, g_ref, *refs):
    nw = len(kinds)
    w_refs, o_refs = refs[:nw], refs[nw:]
    h = _norm_mod(x_ref[...], g_ref[...], mod_ref[1], mod_ref[0]).astype(BF16)
    for kind, w_ref, o_ref in zip(kinds, w_refs, o_refs):
        if kind == "nn":
            width = w_ref.shape[1]
            for c in range(0, width, PROJ_CHUNK):
                e = min(c + PROJ_CHUNK, width)
                o_ref[:, c:e] = jnp.dot(h, w_ref[:, c:e], preferred_element_type=F32).astype(o_ref.dtype)
        else:
            width = w_ref.shape[0]
            for c in range(0, width, PROJ_CHUNK):
                e = min(c + PROJ_CHUNK, width)
                o_ref[c:e, :] = lax.dot_general(w_ref[c:e, :], h, (((1,), (1,)), ((), ())),
                                                preferred_element_type=F32).astype(o_ref.dtype)


def _proj(x, mod, g, weights, kinds, dtypes, tm, name):
    G, R, _ = x.shape
    rm = mod.shape[2]
    grid = (G, R // tm)
    in_specs = [pl.BlockSpec((None, tm, D_MODEL), lambda n, i: (n, i, 0)),
                pl.BlockSpec((None, 6, rm, D_MODEL), lambda n, i: (n, 0, 0, 0)),
                pl.BlockSpec((1, D_MODEL), lambda n, i: (0, 0))]
    out_specs, out_shape = [], []
    for w, kind, dt in zip(weights, kinds, dtypes):
        in_specs.append(pl.BlockSpec(w.shape, lambda n, i: (0, 0)))
        if kind == "nn":
            width = w.shape[1]
            out_specs.append(pl.BlockSpec((None, tm, width), lambda n, i: (n, i, 0)))
            out_shape.append(jax.ShapeDtypeStruct((G, R, width), dt))
        else:
            width = w.shape[0]
            out_specs.append(pl.BlockSpec((None, width, tm), lambda n, i: (n, 0, i)))
            out_shape.append(jax.ShapeDtypeStruct((G, width, R), dt))
    return pl.pallas_call(
        functools.partial(_proj_kernel, tuple(kinds)),
        grid=grid, in_specs=in_specs, out_specs=out_specs, out_shape=out_shape,
        compiler_params=_cparams(("parallel", "parallel")),
        name=name,
    )(x, mod, g, *weights)


def _compress_core(jobs, n_tiles):
    n_chunks = n_tiles * (LANE // CMP_STRIDE)
    per_tile = LANE // CMP_STRIDE
    width = 2 * CMP_HIDDEN
    n_pairs = KV_HEADS // 2
    pe2s = []
    for _, _, _, (_, _, pe2_ref, w1f_ref, _) in jobs:
        pe_term = _dot(pe2_ref[...], w1f_ref[...])[0:1]
        pe2s.append(jnp.concatenate([pe_term, pe_term], axis=1))
    for t in range(n_tiles):
        for tile_fn, kp_ref, _, (perm_ref, _, _, _, _) in jobs:
            zs = _dot_nt(perm_ref[...], tile_fn(t))
            for pair in range(n_pairs):
                r0 = pair * n_chunks + t * per_tile
                for p in range(CMP_STRIDE):
                    kp_ref[r0:r0 + per_tile, p * LANE:(p + 1) * LANE] = (
                        zs[p * per_tile:(p + 1) * per_tile, pair * LANE:(pair + 1) * LANE])
    accs = [_dot(kp_ref[...], w[1][...]) for _, kp_ref, _, w in jobs]
    hids = []
    for pair in range(n_pairs):
        for (_, _, sh_ref, _), acc, pe2 in zip(jobs, accs, pe2s):
            a = acc[pair * n_chunks:(pair + 1) * n_chunks]
            sh_ref[pair, 0:n_chunks, :] = a[:, width:]
            sh_ref[pair, n_chunks:n_chunks + 8, :] = jnp.zeros((8, width), F32)
            hids.append(jax.nn.gelu(a[:, :width] + sh_ref[pair, 1:n_chunks + 1, :] + pe2))
    outs = [[] for _ in jobs]
    for pair in range(n_pairs):
        for gg in range(2):
            for j, (_, _, _, w) in enumerate(jobs):
                hid = hids[pair * len(jobs) + j]
                outs[j].append(_dot_nt(w[4][...], hid[:, gg * CMP_HIDDEN:(gg + 1) * CMP_HIDDEN]))
    return outs


def _compress_kernel(n_tiles, kt_ref, perm_ref, wab_ref, pe2_ref, w1f_ref, w2t_ref, o_ref, kp_ref, sh_ref):
    tile = lambda t: kt_ref[:, t * LANE:(t + 1) * LANE]
    (outs,) = _compress_core([(tile, kp_ref, sh_ref, (perm_ref, wab_ref, pe2_ref, w1f_ref, w2t_ref))], n_tiles)
    for g in range(KV_HEADS):
        o_ref[g] = outs[g]


def _compress_scratch(n_chunks):
    return [pltpu.VMEM(((KV_HEADS // 2) * n_chunks, CMP_STRIDE * LANE), F32),
            pltpu.VMEM((KV_HEADS // 2, n_chunks + 8, 2 * CMP_HIDDEN), F32)]


def _compress_prompt(kt, cw):
    n, _, length = kt.shape
    n_chunks = length // CMP_STRIDE
    const = lambda a: pl.BlockSpec(a.shape, lambda i: (0,) * a.ndim)
    return pl.pallas_call(
        functools.partial(_compress_kernel, length // LANE),
        grid=(n,),
        in_specs=[pl.BlockSpec((None, KV_WIDTH, length), lambda i: (i, 0, 0))] + [const(a) for a in cw],
        out_specs=pl.BlockSpec((None, KV_HEADS, HEAD_DIM, n_chunks), lambda i: (i, 0, 0, 0)),
        out_shape=jax.ShapeDtypeStruct((n, KV_HEADS, HEAD_DIM, n_chunks), F32),
        scratch_shapes=_compress_scratch(n_chunks),
        compiler_params=_cparams(("parallel",)),
        name="compress_prompt",
    )(kt, *cw)


def _compress_weights(pe, w1, w2):
    eye2 = jnp.eye(2, dtype=F32)
    wa = jnp.einsum("pde,ab->padbe", w1[:CMP_STRIDE], eye2).reshape(CMP_STRIDE, LANE, 2 * CMP_HIDDEN)
    wb = jnp.einsum("pde,ab->padbe", w1[CMP_STRIDE:], eye2).reshape(CMP_STRIDE, LANE, 2 * CMP_HIDDEN)
    wab = jnp.concatenate([wa, wb], axis=2).reshape(CMP_STRIDE * LANE, 4 * CMP_HIDDEN)
    pe2 = jnp.zeros((8, CMP_LEN * HEAD_DIM), F32).at[0].set(pe.reshape(-1))
    w1f = w1.reshape(CMP_LEN * HEAD_DIM, CMP_HIDDEN)
    per_tile = LANE // CMP_STRIDE
    perm = np.zeros((LANE, LANE), np.float32)
    for p in range(CMP_STRIDE):
        for i in range(per_tile):
            perm[p * per_tile + i, CMP_STRIDE * i + p] = 1.0
    return (jnp.asarray(perm, dtype=BF16), wab.astype(BF16), pe2, w1f.astype(BF16), w2.T.astype(BF16))


def _overlap_matrix(n_cmp, n_blk, rows, cols):
    c0 = np.arange(n_cmp)[:, None] * CMP_STRIDE
    b0 = np.arange(n_blk)[None, :] * SLC_BLOCK
    inter = np.minimum(c0 + CMP_LEN, b0 + SLC_BLOCK) - np.maximum(c0, b0)
    ov = np.zeros((rows, cols), np.float32)
    ov[:n_cmp, :n_blk] = np.clip(inter, 0, None) / CMP_LEN
    return jnp.asarray(ov)


def _topk_select(imp, n_blk, top_n):
    lane = lax.broadcasted_iota(jnp.int32, imp.shape, 1)
    rank = jnp.zeros(imp.shape, F32)
    for i in range(n_blk):
        col = imp[:, i:i + 1]
        ahead = (col > imp) | ((col == imp) & (lane > i))
        rank = rank + ahead.astype(F32)
    return (rank < float(top_n)).astype(F32)


def _imp_matmul(psum, ov):
    hi = psum.astype(BF16)
    lo = (psum - hi.astype(F32)).astype(BF16)
    ovb = ov.astype(BF16)
    return jnp.dot(hi, ovb, preferred_element_type=F32) + jnp.dot(lo, ovb, preferred_element_type=F32)


NSA_TQ = 256
NSA_TK_SLC = 1024
LOG2E = 1.4426950408889634


def _nsa_prompt_kernel(seq, q_ref, gate_ref, kc_ref, vc_ref, ks_ref, vs_ref, kw_ref, vw_ref, ovt_ref, eb_ref,
                       o_ref):
    tq = NSA_TQ
    g = pl.program_id(1)
    i = pl.program_id(2)
    s0 = i * tq
    n_cmp = seq // CMP_STRIDE - 1
    n_blk = seq // SLC_BLOCK
    top_n = min(SLC_TOPN, n_blk)

    q = q_ref[...].astype(F32) * (HEAD_DIM ** -0.5)
    qh = [q[:, h * HEAD_DIM:(h + 1) * HEAD_DIM].astype(BF16) for h in range(GROUP)]
    q4 = jnp.concatenate(qh, axis=0)
    qpos = s0 + lax.broadcasted_iota(jnp.int32, (tq, 1), 0)

    n_c = kc_ref.shape[1]
    s = _dot(q4, kc_ref[...]).reshape(GROUP, tq, n_c)
    cidx = lax.broadcasted_iota(jnp.int32, (tq, n_c), 1)
    valid = ((cidx * CMP_STRIDE + (CMP_LEN - 1)) <= qpos) & (cidx < n_cmp)
    sm = jnp.where(valid[None], s, NEG)
    m = jnp.max(sm, axis=-1, keepdims=True)
    p = jnp.where(valid[None], jnp.exp(sm - m), 0.0)
    p = p / jnp.maximum(jnp.sum(p, axis=-1, keepdims=True), 1e-30)
    o_cmp = _dot_nt(p.reshape(GROUP * tq, n_c), vc_ref[...])
    psum = p[0] + p[1] + p[2] + p[3]

    hi = psum.astype(BF16)
    lo = (psum - hi.astype(F32)).astype(BF16)
    imp_t = _dot_nt(ovt_ref[...], hi) + _dot_nt(ovt_ref[...], lo)
    nb8 = -(-n_blk // 8) * 8
    imp_t = imp_t[:nb8]
    blk = lax.broadcasted_iota(jnp.int32, (nb8, tq), 0)
    qrow = s0 + lax.broadcasted_iota(jnp.int32, (nb8, tq), 1)
    forced = (blk == qrow // SLC_BLOCK) | (blk == 0)
    causal = ((blk * SLC_BLOCK) <= qrow) & (blk < n_blk)
    imp_t = jnp.where(forced, jnp.inf, jnp.where(causal, imp_t, -jnp.inf))
    rank = jnp.zeros((nb8, tq), F32)
    for j in range(n_blk):
        row = imp_t[j:j + 1, :]
        ahead = (row > imp_t) | ((row == imp_t) & (blk > j))
        rank = rank + ahead.astype(F32)
    half = LANE // 2
    sel_t = jnp.where((rank < float(top_n)) & causal, 0.0, NEG)
    if nb8 < half:
        sel_t = jnp.concatenate([sel_t, jnp.full((half - nb8, tq), NEG, F32)], axis=0)
    selneg = jnp.concatenate([sel_t, sel_t], axis=0).T
    lane_q = lax.broadcasted_iota(jnp.int32, (tq, LANE), 1)
    q2 = q_ref[...].astype(F32) * (HEAD_DIM ** -0.5 * LOG2E)
    q_aug = []
    for h in range(GROUP):
        tile = q2[:, (h // 2) * LANE:(h // 2 + 1) * LANE]
        own = (lane_q < half) if h % 2 == 0 else (lane_q >= half)
        q_aug.append(jnp.where(own, tile, selneg).astype(BF16))
    tk = NSA_TK_SLC
    kt_d = (s0 + tq - 1) // tk

    def slc_tile(kt, span, carry, causal_bias):
        k0 = pl.multiple_of(kt * tk, tk)
        kT = ks_ref[:, pl.ds(k0, span)].astype(BF16)
        eT = eb_ref[:, pl.ds(k0, span)]
        k_aug = (jnp.concatenate([kT, eT], axis=0), jnp.concatenate([eT, kT], axis=0))
        vT = vs_ref[:, pl.ds(k0, span)].astype(BF16)
        if causal_bias:
            kpos = k0 + lax.broadcasted_iota(jnp.int32, (tq, span), 1)
            bias = jnp.where(kpos <= qpos, 0.0, NEG)
        out = []
        for h in range(GROUP):
            m, l, acc = carry[h]
            sm = jnp.dot(q_aug[h], k_aug[h % 2], preferred_element_type=F32)
            if causal_bias:
                sm = sm + bias
            m_new = jnp.maximum(m, jnp.max(sm, axis=-1, keepdims=True))
            alpha = jnp.exp2(m - m_new)
            p = jnp.exp2(sm - m_new)
            l = alpha * l + jnp.sum(p, axis=-1, keepdims=True)
            acc = alpha * acc + _dot_nt(p, vT)
            out.append((m_new, l, acc))
        return tuple(out)

    init = tuple((jnp.full((tq, 1), NEG, F32), jnp.zeros((tq, 1), F32), jnp.zeros((tq, HEAD_DIM), F32))
                 for _ in range(GROUP))
    carry = slc_tile(kt_d, tk, init, True)
    carry = lax.fori_loop(0, kt_d, lambda j, c: slc_tile(kt_d - 1 - j, tk, c, False), carry)
    o_slc = jnp.concatenate([acc / l for _, l, acc in carry], axis=0)

    span = min(WINDOW + tq, seq)
    w0 = pl.multiple_of(jnp.minimum(jnp.maximum(s0 - WINDOW, 0), seq - span), tq)
    kT = kw_ref[:, pl.ds(w0, span)].astype(BF16)
    vT = vw_ref[:, pl.ds(w0, span)].astype(BF16)
    kpos = w0 + lax.broadcasted_iota(jnp.int32, (tq, span), 1)
    bias = jnp.where((kpos <= qpos) & (kpos > qpos - WINDOW), 0.0, NEG)
    o_win = []
    for h in range(GROUP):
        sm = _dot(qh[h], kT) + bias
        p = jnp.exp(sm - jnp.max(sm, axis=-1, keepdims=True))
        o_win.append(_dot_nt(p, vT) / jnp.sum(p, axis=-1, keepdims=True))
    o_win = jnp.concatenate(o_win, axis=0)

    sg = _sigmoid(gate_ref[...])
    lane = lax.broadcasted_iota(jnp.int32, (tq, LANE), 1)
    for h in range(GROUP):
        base = g * (GROUP * 3) + h * 3
        gc = [jnp.sum(jnp.where(lane == base + b, sg, 0.0), axis=-1, keepdims=True) for b in range(3)]
        rows = slice(h * tq, (h + 1) * tq)
        out = gc[0] * o_cmp[rows] + gc[1] * o_slc[rows] + gc[2] * o_win[rows]
        o_ref[:, h * HEAD_DIM:(h + 1) * HEAD_DIM] = out


def _nsa_prompt(q, small, kct, vct, kst, vst, kwt, vwt):
    n, seq, _ = q.shape
    n_c = kct.shape[-1]
    assert NSA_TK_SLC % NSA_TQ == 0 and seq % NSA_TK_SLC == 0 and WINDOW % NSA_TQ == 0
    assert seq // SLC_BLOCK <= LANE // 2
    ov = _overlap_matrix(seq // CMP_STRIDE - 1, seq // SLC_BLOCK, n_c, LANE).T
    eb = np.zeros((LANE // 2, seq), np.float32)
    eb[np.arange(seq) // SLC_BLOCK, np.arange(seq)] = 1.0
    eb = jnp.asarray(eb, dtype=BF16)
    cw = GROUP * HEAD_DIM
    head_spec = lambda width: pl.BlockSpec((None, None, HEAD_DIM, width), lambda b, g, i: (b, g, 0, 0))
    r4 = lambda a: a.reshape(n, KV_HEADS, HEAD_DIM, seq)
    return pl.pallas_call(
        functools.partial(_nsa_prompt_kernel, seq),
        grid=(n, KV_HEADS, seq // NSA_TQ),
        in_specs=[pl.BlockSpec((None, NSA_TQ, cw), lambda b, g, i: (b, i, g)),
                  pl.BlockSpec((None, NSA_TQ, LANE), lambda b, g, i: (b, i, 0)),
                  head_spec(n_c), head_spec(n_c), head_spec(seq), head_spec(seq), head_spec(seq), head_spec(seq),
                  pl.BlockSpec(ov.shape, lambda b, g, i: (0, 0)),
                  pl.BlockSpec(eb.shape, lambda b, g, i: (0, 0))],
        out_specs=pl.BlockSpec((None, NSA_TQ, cw), lambda b, g, i: (b, i, g)),
        out_shape=jax.ShapeDtypeStruct((n, seq, D_MODEL), F32),
        compiler_params=_cparams(("parallel", "parallel", "parallel")),
        name="nsa_prompt",
    )(q, small, kct, vct, r4(kst), r4(vst), r4(kwt), r4(vwt), ov, eb)


DN_C = 128


def _dn_gates(small, alog_ref, dt_ref):
    g = -jnp.exp(alog_ref[...]) * _softplus(small + dt_ref[...])
    return g, _sigmoid(small)


def _l2n(x):
    return x * lax.rsqrt(jnp.sum(x * x, axis=-1, keepdims=True) + EPS)


def _dn_out(o, z, ng):
    on = o * lax.rsqrt(jnp.mean(o * o, axis=-1, keepdims=True) + EPS) * ng
    return on * (z * _sigmoid(z))


INV_BASE = 16


def _unit_lower_inverse(lmats, eye, ri, ci, c):
    same = (ri // INV_BASE) == (ci // INV_BASE)
    pws = [jnp.where(same, -l, 0.0) for l in lmats]
    xs = [eye + pw for pw in pws]
    pws = [_dot(pw, pw) for pw in pws]
    steps = int(np.log2(INV_BASE)) - 1
    for k in range(steps):
        if k < steps - 1:
            rs = [_dot(pw, jnp.concatenate([pw, x], axis=1)) for pw, x in zip(pws, xs)]
            xs = [x + r[:, c:] for x, r in zip(xs, rs)]
            pws = [r[:, :c] for r in rs]
        else:
            xs = [x + _dot(pw, x) for x, pw in zip(xs, pws)]
    s = INV_BASE
    while s < c:
        off = ((ri // (2 * s)) == (ci // (2 * s))) & ((ri // s) != (ci // s))
        ts = [_dot(x, jnp.where(off, l, 0.0)) for x, l in zip(xs, lmats)]
        xs = [x - _dot(t, x) for x, t in zip(xs, ts)]
        s *= 2
    return xs


def _dn_prompt_kernel(qkv_ref, z_ref, small_ref, cw_ref, alog_ref, dt_ref, ng_ref, o_ref, st_ref, xbuf, s_ref):
    c = DN_C
    i = pl.program_id(1)
    last = pl.num_programs(1) - 1

    @pl.when(i == 0)
    def _():
        xbuf[0:8, :] = jnp.zeros((8, DN_QKV), F32)
        s_ref[...] = jnp.zeros_like(s_ref)

    xbuf[8:8 + c, :] = qkv_ref[...]
    y = (cw_ref[3:4, :] * xbuf[8:8 + c, :] + cw_ref[2:3, :] * xbuf[7:7 + c, :]
         + cw_ref[1:2, :] * xbuf[6:6 + c, :] + cw_ref[0:1, :] * xbuf[5:5 + c, :])
    xbuf[0:8, :] = xbuf[c:c + 8, :]
    y = y * _sigmoid(y)

    gl, bt = _dn_gates(small_ref[...], alog_ref, dt_ref)
    ri = lax.broadcasted_iota(jnp.int32, (c, c), 0)
    ci = lax.broadcasted_iota(jnp.int32, (c, c), 1)
    incl = ri >= ci
    strict = ri > ci
    gc_col = _dot_exact_lhs(incl.astype(F32), gl)
    gc_row = _dot_exact_rhs(gl.T, (ri <= ci).astype(F32))
    eye = (ri == ci).astype(F32)
    nd = DN_HEADS * DN_DK

    hs = range(DN_HEADS)
    sls = [slice(h * DN_DK, (h + 1) * DN_DK) for h in hs]
    q = [_l2n(y[:, sl]) * (DN_DK ** -0.5) for sl in sls]
    k = [_l2n(y[:, nd + h * DN_DK:nd + (h + 1) * DN_DK]) for h in hs]
    v = [y[:, 2 * nd + h * DN_DK:2 * nd + (h + 1) * DN_DK] for h in hs]
    gcc = [gc_col[:, A_LANE + h:A_LANE + h + 1] for h in hs]
    gcr = [gc_row[A_LANE + h:A_LANE + h + 1, :] for h in hs]
    beta = [bt[:, B_LANE + h:B_LANE + h + 1] for h in hs]
    decay = [jnp.exp(jnp.where(incl, gcc[h] - gcr[h], NEG)) for h in hs]
    kb = [k[h] * beta[h] for h in hs]
    kq = [_dot_nt(jnp.concatenate([kb[h], q[h]], axis=0), k[h]) for h in hs]
    lmat = [jnp.where(strict, kq[h][:c] * decay[h], 0.0) for h in hs]
    a_intra = [kq[h][c:] * decay[h] for h in hs]
    tmat = _unit_lower_inverse(lmat, eye, ri, ci, c)
    eg = [jnp.exp(gcc[h]) for h in hs]
    uw = [_dot(tmat[h], jnp.concatenate([v[h] * beta[h], kb[h] * eg[h]], axis=1)) for h in hs]
    g_last = [gcr[h][:, c - 1:c] for h in hs]
    k_dec_t = [(k[h] * jnp.exp(g_last[h] - gcc[h])).T for h in hs]
    state = [s_ref[h] for h in hs]
    ws = [_dot(jnp.concatenate([uw[h][:, DN_DK:], q[h] * eg[h]], axis=0), state[h]) for h in hs]
    v_new = [uw[h][:, :DN_DK] - ws[h][:c] for h in hs]
    o = [ws[h][c:] + _dot(a_intra[h], v_new[h]) for h in hs]
    for h in hs:
        s_ref[h] = state[h] * jnp.exp(g_last[h]) + _dot(k_dec_t[h], v_new[h])
    for h in hs:
        o_ref[:, sls[h]] = _dn_out(o[h], z_ref[:, sls[h]], ng_ref[...])

    @pl.when(i == last)
    def _():
        st_ref[...] = s_ref[...]


def _dn_prompt(qkv, z, small, conv_w, alog_row, dt_row, norm_g):
    n, seq, _ = qkv.shape
    c = DN_C
    const = lambda a: pl.BlockSpec(a.shape, lambda b, i: (0,) * a.ndim)
    return pl.pallas_call(
        _dn_prompt_kernel,
        grid=(n, seq // c),
        in_specs=[pl.BlockSpec((None, c, DN_QKV), lambda b, i: (b, i, 0)),
                  pl.BlockSpec((None, c, D_MODEL), lambda b, i: (b, i, 0)),
                  pl.BlockSpec((None, c, LANE), lambda b, i: (b, i, 0)),
                  const(conv_w), const(alog_row), const(dt_row), const(norm_g)],
        out_specs=[pl.BlockSpec((None, c, D_MODEL), lambda b, i: (b, i, 0)),
                   pl.BlockSpec((None, DN_HEADS, DN_DK, DN_DK), lambda b, i: (b, 0, 0, 0))],
        out_shape=[jax.ShapeDtypeStruct((n, seq, D_MODEL), F32),
                   jax.ShapeDtypeStruct((n, DN_HEADS, DN_DK, DN_DK), F32)],
        scratch_shapes=[pltpu.VMEM((c + 8, DN_QKV), F32), pltpu.VMEM((DN_HEADS, DN_DK, DN_DK), F32)],
        compiler_params=_cparams(("parallel", "arbitrary")),
        name="dn_prompt",
    )(qkv, z, small, conv_w, alog_row, dt_row, norm_g)


MLP_TF = 2048


def _mlp_kernel(x_ref, on_ref, od_ref, mg_ref, mod_ref, g2_ref, gf_ref, wo_ref, wu_ref, wd_ref, y_ref,
                x1_ref, h2_ref, acc_ref):
    j = pl.program_id(2)
    last = pl.num_programs(2) - 1

    @pl.when(j == 0)
    def _():
        ga = _sigmoid(mg_ref[:, :D_MODEL])
        gb = _sigmoid(mg_ref[:, D_MODEL:])
        mixed = ga * on_ref[...] + gb * od_ref[...]
        x1 = x_ref[...] + mod_ref[2] * _dot(mixed, wo_ref[...])
        x1_ref[...] = x1
        h2_ref[...] = _norm_mod(x1, g2_ref[...], mod_ref[4], mod_ref[3]).astype(BF16)
        acc_ref[...] = jnp.zeros_like(acc_ref)

    up = jnp.dot(h2_ref[...], wu_ref[...], preferred_element_type=F32)
    act = jnp.square(jnp.maximum(up, 0.0))
    acc_ref[...] += _dot(act, wd_ref[...])

    @pl.when(j == last)
    def _():
        x2 = x1_ref[...] + mod_ref[5] * acc_ref[...]
        y_ref[...] = x2 * lax.rsqrt(jnp.mean(x2 * x2, axis=-1, keepdims=True) + EPS) * gf_ref[...]


def _mlp(x, o_nsa, o_dn, merge, mod, g2, gf, w_out, w_up, w_down, tm, name):
    G, R, _ = x.shape
    rm = mod.shape[2]
    tf = MLP_TF
    row = lambda width: pl.BlockSpec((None, tm, width), lambda n, i, j: (n, i, 0))
    return pl.pallas_call(
        _mlp_kernel,
        grid=(G, R // tm, D_FF // tf),
        in_specs=[row(D_MODEL), row(D_MODEL), row(D_MODEL), row(2 * D_MODEL),
                  pl.BlockSpec((None, 6, rm, D_MODEL), lambda n, i, j: (n, 0, 0, 0)),
                  pl.BlockSpec((1, D_MODEL), lambda n, i, j: (0, 0)),
                  pl.BlockSpec((1, D_MODEL), lambda n, i, j: (0, 0)),
                  pl.BlockSpec((D_MODEL, D_MODEL), lambda n, i, j: (0, 0)),
                  pl.BlockSpec((D_MODEL, tf), lambda n, i, j: (0, j)),
                  pl.BlockSpec((tf, D_MODEL), lambda n, i, j: (j, 0))],
        out_specs=row(D_MODEL),
        out_shape=jax.ShapeDtypeStruct((G, R, D_MODEL), F32),
        scratch_shapes=[pltpu.VMEM((tm, D_MODEL), F32), pltpu.VMEM((tm, D_MODEL), BF16),
                        pltpu.VMEM((tm, D_MODEL), F32)],
        compiler_params=_cparams(("parallel", "parallel", "arbitrary")),
        name=name,
    )(x, o_nsa, o_dn, merge, mod, g2, gf, w_out, w_up, w_down)


def _nsa_sample_kernel(n_pages, page, win, *refs):
    n_in = 4 * n_pages
    pt_ref = refs[0]
    del pt_ref
    pools = [refs[1 + t * n_pages:1 + (t + 1) * n_pages] for t in range(4)]
    (q_ref, gate_ref, newt_ref, kwc_ref, vwc_ref, ov_ref, ex_ref) = refs[1 + n_in:1 + n_in + 7]
    rows16 = newt_ref[...]
    new_t = jnp.concatenate([rows16, jnp.zeros((LANE - rows16.shape[0], LANE), F32)], axis=0).T
    new_ref = [jnp.concatenate([new_t[:, 2 * t:2 * t + 1], new_t[:, 2 * t + 1:2 * t + 2]], axis=0)
               for t in range(6)]
    cwk = refs[1 + n_in + 7:1 + n_in + 12]
    cwv = refs[1 + n_in + 12:1 + n_in + 17]
    o_ref, kwo_ref, vwo_ref = refs[1 + n_in + 17:1 + n_in + 20]
    full_a, full_b, kpk_ref, shk_ref, kpv_ref, shv_ref = refs[1 + n_in + 20:]

    past = n_pages * page
    total = past + 1
    lp = -(-total // SLC_BLOCK) * SLC_BLOCK
    n_cmp = lp // CMP_STRIDE - 1
    n_blk = lp // SLC_BLOCK
    top_n = min(SLC_TOPN, n_blk)
    length = full_a.shape[1]
    n_c = length // CMP_STRIDE
    n_tiles = length // LANE

    def fill(full_ref, page_refs, t):
        for j in range(n_pages):
            full_ref[:, j * page:(j + 1) * page] = page_refs[j][...]
        full_ref[:, past:length] = jnp.zeros((KV_WIDTH, length - past), F32)
        full_ref[:, past:past + 1] = new_ref[t]

    def cmp_tiles(page_refs, t):
        def tile(j):
            if j < n_pages:
                return page_refs[j][...]
            if j == n_pages:
                lane = lax.broadcasted_iota(jnp.int32, (KV_WIDTH, LANE), 1)
                return jnp.where(lane == 0, new_ref[t], 0.0)
            return jnp.zeros((KV_WIDTH, LANE), F32)
        return tile

    q16 = (q_ref[...].astype(F32) * (HEAD_DIM ** -0.5)).astype(BF16)
    head_group = lax.broadcasted_iota(jnp.int32, (N_HEADS, 1), 0) // GROUP

    def by_group(vals):
        out = vals[KV_HEADS - 1]
        for g in range(KV_HEADS - 2, -1, -1):
            out = jnp.where(head_group == g, vals[g], out)
        return out

    def attend(kt, vt, valid):
        rows = [slice(g * HEAD_DIM, (g + 1) * HEAD_DIM) for g in range(KV_HEADS)]
        s = by_group([_dot(q16, kt[r]) for r in rows])
        sm = jnp.where(valid, s, NEG)
        m = jnp.max(sm, axis=-1, keepdims=True)
        p = jnp.where(valid, jnp.exp(sm - m), 0.0)
        p = p / jnp.maximum(jnp.sum(p, axis=-1, keepdims=True), 1e-30)
        o = by_group([_dot_nt(p, vt[r]) for r in rows])
        return p, o

    lane_w = lax.broadcasted_iota(jnp.int32, (KV_WIDTH, win), 1)
    kw = jnp.where(lane_w == win - 1, new_ref[4], pltpu.roll(kwc_ref[...], win - 1, axis=1))
    vw = jnp.where(lane_w == win - 1, new_ref[5], pltpu.roll(vwc_ref[...], win - 1, axis=1))
    kwo_ref[...] = kw
    vwo_ref[...] = vw
    wpos = past - win + 1 + lax.broadcasted_iota(jnp.int32, (1, win), 1)
    valid_w = (wpos <= past) & (wpos > past - WINDOW) & (wpos >= 0)
    _, o_win = attend(kw, vw, valid_w)
    fill(full_a, pools[2], 2)
    fill(full_b, pools[3], 3)

    kcs, vcs = _compress_core([(cmp_tiles(pools[0], 0), kpk_ref, shk_ref, cwk),
                               (cmp_tiles(pools[1], 1), kpv_ref, shv_ref, cwv)], n_tiles)
    cidx = lax.broadcasted_iota(jnp.int32, (1, n_c), 1)
    valid_c = ((cidx * CMP_STRIDE + (CMP_LEN - 1)) <= past) & (cidx < n_cmp)
    p_cmp, o_cmp = attend(jnp.concatenate(kcs, axis=0), jnp.concatenate(vcs, axis=0), valid_c)
    imp_h = _imp_matmul(p_cmp, ov_ref[...])
    hr = lax.broadcasted_iota(jnp.int32, (N_HEADS, N_HEADS), 0) // GROUP
    hc = lax.broadcasted_iota(jnp.int32, (N_HEADS, N_HEADS), 1) // GROUP
    imp = _dot_exact_lhs((hr == hc).astype(F32), imp_h)
    blk = lax.broadcasted_iota(jnp.int32, (N_HEADS, LANE), 1)
    forced = (blk == past // SLC_BLOCK) | (blk == 0)
    causal = (blk * SLC_BLOCK) <= past
    imp = jnp.where(forced, jnp.inf, jnp.where(causal & (blk < n_blk), imp, -jnp.inf))
    sel = _topk_select(imp, n_blk, top_n).astype(BF16)

    kpos = lax.broadcasted_iota(jnp.int32, (1, length), 1)
    chosen = jnp.dot(sel, ex_ref[...], preferred_element_type=F32) > 0.5
    valid_s = chosen & (kpos <= past)
    _, o_slc = attend(full_a[...], full_b[...], valid_s)

    sg = _sigmoid(gate_ref[...])
    o_ref[...] = sg[:, 0:1] * o_cmp + sg[:, 1:2] * o_slc + sg[:, 2:3] * o_win


def _nsa_sample(page_table, pools, q16, gates, newcols, kwc, vwc, cwk, cwv):
    n_seq, n_pages = page_table.shape
    page = pools[0].shape[-1]
    win = kwc.shape[-1]
    past = n_pages * page
    lp = -(-(past + 1) // SLC_BLOCK) * SLC_BLOCK
    length = -(-lp // LANE) * LANE
    n_c = length // CMP_STRIDE
    ov = _overlap_matrix(lp // CMP_STRIDE - 1, lp // SLC_BLOCK, n_c, LANE)
    ex = np.zeros((LANE, length), np.float32)
    ex[np.arange(length) // SLC_BLOCK, np.arange(length)] = 1.0
    ex = jnp.asarray(ex, dtype=BF16)
    assert win == WINDOW and page == LANE

    in_specs = []
    for _ in range(4):
        for j in range(n_pages):
            in_specs.append(pl.BlockSpec((None, KV_WIDTH, page), functools.partial(
                lambda s, pt, jj: (pt[s, jj], 0, 0), jj=j)))
    const = lambda a: pl.BlockSpec(a.shape, lambda s, pt: (0,) * a.ndim)
    in_specs += [pl.BlockSpec((None, N_HEADS, HEAD_DIM), lambda s, pt: (s, 0, 0)),
                 pl.BlockSpec((None, N_HEADS, 3), lambda s, pt: (s, 0, 0)),
                 pl.BlockSpec((None,) + newcols.shape[1:], lambda s, pt: (s, 0, 0)),
                 pl.BlockSpec((None, KV_WIDTH, win), lambda s, pt: (s, 0, 0)),
                 pl.BlockSpec((None, KV_WIDTH, win), lambda s, pt: (s, 0, 0)),
                 const(ov), const(ex)] + [const(a) for a in cwk] + [const(a) for a in cwv]
    operands = []
    for t in range(4):
        operands += [pools[t]] * n_pages
    operands += [q16, gates, newcols, kwc, vwc, ov, ex, *cwk, *cwv]
    win_spec = pl.BlockSpec((None, KV_WIDTH, win), lambda s, pt: (s, 0, 0))
    grid_spec = pltpu.PrefetchScalarGridSpec(
        num_scalar_prefetch=1, grid=(n_seq,), in_specs=in_specs,
        out_specs=[pl.BlockSpec((None, N_HEADS, HEAD_DIM), lambda s, pt: (s, 0, 0)), win_spec, win_spec],
        scratch_shapes=[pltpu.VMEM((KV_WIDTH, length), F32), pltpu.VMEM((KV_WIDTH, length), F32)]
        + _compress_scratch(n_c) + _compress_scratch(n_c))
    return pl.pallas_call(
        functools.partial(_nsa_sample_kernel, n_pages, page, win),
        grid_spec=grid_spec,
        out_shape=[jax.ShapeDtypeStruct((n_seq, N_HEADS, HEAD_DIM), F32),
                   jax.ShapeDtypeStruct((n_seq, KV_WIDTH, win), F32),
                   jax.ShapeDtypeStruct((n_seq, KV_WIDTH, win), F32)],
        compiler_params=_cparams(("arbitrary",)),
        name="nsa_sample",
    )(page_table, *operands)


DN_SB = 8


def _dn_sample_kernel(qkv_ref, conv_ref, z_ref, small_ref, st_ref, cw_ref, alog_ref, dt_ref, ng_ref,
                      o_ref, conv_o_ref, st_o_ref):
    x = qkv_ref[...]
    y = (cw_ref[0:1, :] * conv_ref[0] + cw_ref[1:2, :] * conv_ref[1] + cw_ref[2:3, :] * conv_ref[2]
         + cw_ref[3:4, :] * x)
    conv_o_ref[0] = conv_ref[1]
    conv_o_ref[1] = conv_ref[2]
    conv_o_ref[2] = x
    y = y * _sigmoid(y)
    gl, bt = _dn_gates(small_ref[...], alog_ref, dt_ref)
    nd = DN_HEADS * DN_DK
    ri = lax.broadcasted_iota(jnp.int32, (DN_DK, DN_DK), 0)
    ci = lax.broadcasted_iota(jnp.int32, (DN_DK, DN_DK), 1)
    eye = ri == ci
    for h in range(DN_HEADS):
        sl = slice(h * DN_DK, (h + 1) * DN_DK)
        q = _l2n(y[:, sl]) * (DN_DK ** -0.5)
        k = _l2n(y[:, nd + h * DN_DK:nd + (h + 1) * DN_DK])
        v = y[:, 2 * nd + h * DN_DK:2 * nd + (h + 1) * DN_DK]
        g = gl[:, A_LANE + h:A_LANE + h + 1]
        beta = bt[:, B_LANE + h:B_LANE + h + 1]
        eg = jnp.exp(g)
        u = v * beta
        w = k * beta * eg
        a_intra = jnp.sum(q * k, axis=-1, keepdims=True)
        q_dec = q * eg
        z = z_ref[:, sl]
        for b in range(DN_SB):
            state = st_ref[b, h]
            lhs = jnp.concatenate([w[b:b + 1], q_dec[b:b + 1], jnp.zeros((6, DN_DK), F32)], axis=0)
            res = _dot(lhs, state)
            v_new = u[b:b + 1] - res[0:1]
            o = res[1:2] + a_intra[b:b + 1] * v_new
            k_col = jnp.sum(jnp.where(eye, k[b:b + 1], 0.0), axis=-1, keepdims=True)
            st_o_ref[b, h] = state * eg[b:b + 1] + k_col * v_new
            o_ref[b:b + 1, sl] = _dn_out(o, z[b:b + 1], ng_ref[...])


def _dn_sample(qkv, conv_t, z, small, state, conv_w, alog_row, dt_row, norm_g):
    n_seq = qkv.shape[0]
    sb = DN_SB
    const = lambda a: pl.BlockSpec(a.shape, lambda i: (0,) * a.ndim)
    return pl.pallas_call(
        _dn_sample_kernel,
        grid=(n_seq // sb,),
        in_specs=[pl.BlockSpec((sb, DN_QKV), lambda i: (i, 0)),
                  pl.BlockSpec((CONV_W - 1, sb, DN_QKV), lambda i: (0, i, 0)),
                  pl.BlockSpec((sb, D_MODEL), lambda i: (i, 0)),
                  pl.BlockSpec((sb, LANE), lambda i: (i, 0)),
                  pl.BlockSpec((sb, DN_HEADS, DN_DK, DN_DK), lambda i: (i, 0, 0, 0)),
                  const(conv_w), const(alog_row), const(dt_row), const(norm_g)],
        out_specs=[pl.BlockSpec((sb, D_MODEL), lambda i: (i, 0)),
                   pl.BlockSpec((CONV_W - 1, sb, DN_QKV), lambda i: (0, i, 0)),
                   pl.BlockSpec((sb, DN_HEADS, DN_DK, DN_DK), lambda i: (i, 0, 0, 0))],
        out_shape=[jax.ShapeDtypeStruct((n_seq, D_MODEL), F32),
                   jax.ShapeDtypeStruct((CONV_W - 1, n_seq, DN_QKV), F32),
                   jax.ShapeDtypeStruct(state.shape, F32)],
        compiler_params=_cparams(("parallel",)),
        name="dn_sample",
    )(qkv, conv_t, z, small, state, conv_w, alog_row, dt_row, norm_g)


def _pick_tile(rows, pref):
    t = min(rows, pref)
    while rows % t:
        t //= 2
    return t


def _layer_weights(w_in):
    w = w_in.astype(BF16)
    o = 0
    wq = w[:, o:o + D_MODEL]; o += D_MODEL
    wkv = [w[:, o + t * KV_WIDTH:o + (t + 1) * KV_WIDTH].T for t in range(6)]; o += 6 * KV_WIDTH
    wg = w[:, o:o + 3 * N_HEADS]; o += 3 * N_HEADS
    wqkv = w[:, o:o + DN_QKV]; o += DN_QKV
    wz = w[:, o:o + D_MODEL]; o += D_MODEL
    wa = w[:, o:o + DN_HEADS]; o += DN_HEADS
    wb = w[:, o:o + DN_HEADS]; o += DN_HEADS
    wm = w[:, o:o + 2 * D_MODEL]
    wsmall = jnp.concatenate([wg, wa, wb, jnp.zeros((D_MODEL, LANE - B_LANE - DN_HEADS), BF16)], axis=1)
    return wq, wkv, wqkv, wz, wm, wsmall


def kernel(x_prompt, x_sample, c_prompt, c_sample, cache_k_cmp, cache_v_cmp, cache_k_slc, cache_v_slc, cache_k_win, cache_v_win, state_conv, state_dn, page_table, w_ada, b_ada, norm1_g, norm2_g, w_in, cmp_pe_k, cmp_w1_k, cmp_w2_k, cmp_pe_v, cmp_w1_v, cmp_w2_v, dn_conv_w, dn_a_log, dn_dt_bias, dn_norm_g, w_out, w_up, w_down, final_g):
    assert w_ada.shape[0] == 1, "single-layer stack"
    nb, seq, _ = x_prompt.shape
    ns = x_sample.shape[0]
    assert x_sample.shape[1] == 1

    n_c = nb + ns
    rows = -(-n_c // 8) * 8
    c_all = jnp.concatenate([c_prompt, c_sample, jnp.zeros((rows - n_c, D_MODEL), F32)], axis=0)
    mod = _ada(c_all, w_ada[0], b_ada)
    mod_p = mod[:nb].reshape(nb, 6, 1, D_MODEL)
    mod_s = mod[nb:n_c].reshape(ns, 6, D_MODEL).transpose(1, 0, 2)[None]

    wq, wkv, wqkv, wz, wm, wsmall = _layer_weights(w_in[0])
    g1 = norm1_g
    g2 = norm2_g
    gf = final_g.reshape(1, D_MODEL)
    alog_row = jnp.zeros((1, LANE), F32).at[0, A_LANE:A_LANE + DN_HEADS].set(dn_a_log[0])
    dt_row = jnp.zeros((1, LANE), F32).at[0, A_LANE:A_LANE + DN_HEADS].set(dn_dt_bias[0])
    conv_w = dn_conv_w[0]
    cwk = _compress_weights(cmp_pe_k[0], cmp_w1_k[0], cmp_w2_k[0])
    cwv = _compress_weights(cmp_pe_v[0], cmp_w1_v[0], cmp_w2_v[0])
    wo = w_out[0].astype(BF16)
    wu = w_up[0].astype(BF16)
    wd = w_down[0].astype(BF16)

    def project(x, modg, tm, tag):
        q, *kvt = _proj(x, modg, g1, [wq] + wkv, ["nn"] + ["nt"] * 6, [BF16] + [F32] * 6, tm, "proj_a_" + tag)
        (qkv,) = _proj(x, modg, g1, [wqkv], ["nn"], [F32], tm, "proj_b_" + tag)
        z, merge, small = _proj(x, modg, g1, [wz, wm, wsmall], ["nn"] * 3, [F32] * 3, tm, "proj_c_" + tag)
        return q, kvt, qkv, z, merge, small

    def kv_out(a):
        n, _, length = a.shape
        return a.reshape(n, KV_HEADS, HEAD_DIM, length).transpose(0, 3, 1, 2)[None]

    tm_p = _pick_tile(seq, 512)
    q, kvt, qkv, z, merge, small = project(x_prompt, mod_p, tm_p, "p")
    kct = _compress_prompt(kvt[0], cwk)
    vct = _compress_prompt(kvt[1], cwv)
    o_nsa = _nsa_prompt(q, small, kct, vct, kvt[2], kvt[3], kvt[4], kvt[5])
    o_dn, p_dn = _dn_prompt(qkv, z, small, conv_w, alog_row, dt_row, dn_norm_g)
    y_prompt = _mlp(x_prompt, o_nsa, o_dn, merge, mod_p, g2, gf, wo, wu, wd, tm_p, "mlp_p")
    wlen = min(WINDOW, seq)
    p_conv = qkv[:, seq - (CONV_W - 1):, :]

    xs = x_sample.reshape(1, ns, D_MODEL)
    qs, kvts, qkvs, zs, merges, smalls = project(xs, mod_s, ns, "s")
    pools = [c[0].transpose(0, 2, 3, 1).reshape(c.shape[1], KV_WIDTH, c.shape[2])
             for c in (cache_k_cmp, cache_v_cmp, cache_k_slc, cache_v_slc)]
    wins = [c[0].transpose(0, 2, 3, 1).reshape(ns, KV_WIDTH, c.shape[2]) for c in (cache_k_win, cache_v_win)]
    newrows = jnp.stack([a[0].T.reshape(ns, KV_WIDTH // LANE, LANE) for a in kvts], axis=1)
    newcols = jnp.pad(newrows.reshape(ns, 6 * (KV_WIDTH // LANE), LANE), ((0, 0), (0, 4), (0, 0)))
    q16 = qs.reshape(ns, N_HEADS, HEAD_DIM)
    gates = smalls[0, :, :3 * N_HEADS].reshape(ns, N_HEADS, 3)
    o_nsa_s, kw_new, vw_new = _nsa_sample(page_table, pools, q16, gates, newcols, wins[0], wins[1], cwk, cwv)
    conv_t = state_conv[0].transpose(1, 0, 2)
    o_dn_s, conv_new, s_dn = _dn_sample(qkvs[0], conv_t, zs[0], smalls[0], state_dn[0], conv_w, alog_row,
                                        dt_row, dn_norm_g)
    y_sample = _mlp(xs, o_nsa_s.reshape(1, ns, D_MODEL), o_dn_s[None], merges, mod_s, g2, gf, wo, wu, wd,
                    ns, "mlp_s")
    s_win = [kw_new, vw_new]

    return (y_prompt, y_sample.reshape(ns, 1, D_MODEL),
            kv_out(kvt[0]), kv_out(kvt[1]), kv_out(kvt[2]), kv_out(kvt[3]),
            kv_out(kvt[4][:, :, seq - wlen:]), kv_out(kvt[5][:, :, seq - wlen:]),
            p_conv[None], p_dn[None],
            kvts[0][0].T.reshape(ns, 1, KV_HEADS, HEAD_DIM)[None],
            kvts[1][0].T.reshape(ns, 1, KV_HEADS, HEAD_DIM)[None],
            kvts[2][0].T.reshape(ns, 1, KV_HEADS, HEAD_DIM)[None],
            kvts[3][0].T.reshape(ns, 1, KV_HEADS, HEAD_DIM)[None],
            kv_out(s_win[0]), kv_out(s_win[1]),
            conv_new.transpose(1, 0, 2)[None], s_dn[None])
```
